```python
import math
import jax, jax.numpy as jnp
from jax import lax
import numpy as np

D_MODEL = 2048
BATCH = 1
SEQ = 8192
DEPTH = 1

MIX_WIDTH = D_MODEL
ATTN_WIDTH = MIX_WIDTH // 2
GMLP_WIDTH = MIX_WIDTH - ATTN_WIDTH
N_DIFF_HEADS = 8
DIFF_HEAD_DIM = ATTN_WIDTH // (2 * N_DIFF_HEADS)
DIFF_V_DIM = 2 * DIFF_HEAD_DIM
N_GMLP_GROUPS = 8
GMLP_GROUP_DIM = GMLP_WIDTH // N_GMLP_GROUPS
CHUNK = 128
Q_BLOCK = 128
ROPE_THETA = 10000.0
D_FF = -(-8 * D_MODEL // (3 * 256)) * 256
RMS_EPS = 1e-6
LN_EPS = 1e-5
SUBLN_EPS = 1e-5
Q_WIDTH = 2 * N_DIFF_HEADS * DIFF_HEAD_DIM
K_WIDTH = Q_WIDTH
V_WIDTH = N_DIFF_HEADS * DIFF_V_DIM
IN_PROJ_WIDTH = Q_WIDTH + K_WIDTH + V_WIDTH + 2 * GMLP_WIDTH

kernel_name = "hybrid_diffattn_gmlp_encoder_layer"


def _rmsnorm(x, g, eps=RMS_EPS):
    xf = x.astype(jnp.float32)
    y = xf * lax.rsqrt(jnp.mean(xf * xf, axis=-1, keepdims=True) + eps)
    return (y * g.astype(jnp.float32)).astype(x.dtype)


def _layernorm(x, g, b, eps=LN_EPS):
    xf = x.astype(jnp.float32)
    mu = jnp.mean(xf, axis=-1, keepdims=True)
    var = jnp.mean(jnp.square(xf - mu), axis=-1, keepdims=True)
    y = (xf - mu) * lax.rsqrt(var + eps)
    return (y * g.astype(jnp.float32) + b.astype(jnp.float32)).astype(x.dtype)


def _rope_tables(positions):
    inv_freq = ROPE_THETA ** (-jnp.arange(0, DIFF_HEAD_DIM, 2, dtype=jnp.float32) / DIFF_HEAD_DIM)
    ang = positions.astype(jnp.float32)[..., None] * inv_freq
    ang = jnp.concatenate([ang, ang], axis=-1)[:, :, None, :]
    return jnp.cos(ang), jnp.sin(ang)


def _apply_rope(x, cos, sin):
    xf = x.astype(jnp.float32)
    x1, x2 = jnp.split(xf, 2, axis=-1)
    rot = jnp.concatenate([-x2, x1], axis=-1)
    return (xf * cos + rot * sin).astype(x.dtype)


def _diff_attention(q, k, v, lam, lambda_init, subln_g):
    B, S = q.shape[0], q.shape[1]
    H, hd, dv = N_DIFF_HEADS, DIFF_HEAD_DIM, DIFF_V_DIM
    nblk = S // Q_BLOCK
    q = q.reshape(B, S, H, 2, hd).transpose(0, 2, 3, 1, 4)
    k = k.reshape(B, S, H, 2, hd).transpose(0, 2, 3, 1, 4)
    v = v.transpose(0, 2, 1, 3)
    qb = q.reshape(B, H, 2, nblk, Q_BLOCK, hd).transpose(3, 0, 1, 2, 4, 5)
    scale = hd ** -0.5

    def one_block(q_blk):
        s = jnp.einsum('bhiqd,bhikd->bhiqk', q_blk, k,
                       preferred_element_type=jnp.float32) * scale
        p = jax.nn.softmax(s, axis=-1)
        w = p[:, :, 0] - lam * p[:, :, 1]
        return jnp.einsum('bhqk,bhkd->bhqd', w.astype(v.dtype), v)

    out = lax.map(one_block, qb)
    out = out.transpose(1, 0, 3, 2, 4).reshape(B, S, H, dv)
    out = _rmsnorm(out, subln_g, SUBLN_EPS) * (1.0 - lambda_init)
    return out.reshape(B, S, H * dv)


def _spatial_gating(u, v, ln_g, ln_b, w_s, b_s):
    B, S = u.shape[0], u.shape[1]
    n = S // CHUNK
    v = _layernorm(v, ln_g, ln_b)
    vc = v.reshape(B, n, CHUNK, N_GMLP_GROUPS, GMLP_GROUP_DIM)
    y = jnp.einsum('gpq,bnqgc->bnpgc', w_s, vc) + b_s.T[None, None, :, :, None]
    return u * y.reshape(B, S, GMLP_WIDTH)


def setup_inputs(seed: int = 0) -> dict:
    key = jax.random.key(seed)
    ks = jax.random.split(key, 20)
    f32 = jnp.float32
    nrm = lambda k, shape, s: jax.random.normal(k, shape, f32) * s
    gain = lambda k, shape: 1.0 + 0.01 * jax.random.normal(k, shape, f32)
    x = jax.random.normal(ks[0], (BATCH, SEQ, D_MODEL), f32)
    offset = jax.random.randint(ks[1], (BATCH, 1), 0, 1024, dtype=jnp.int32)
    positions = jnp.arange(SEQ, dtype=jnp.int32)[None, :] + offset
    return {
        "x": x,
        "positions": positions,
        "pre_mix_g": gain(ks[2], (DEPTH, D_MODEL)),
        "w_in": nrm(ks[3], (DEPTH, D_MODEL, IN_PROJ_WIDTH), D_MODEL ** -0.5),
        "lambda_q1": nrm(ks[4], (DEPTH, DIFF_HEAD_DIM), 0.1),
        "lambda_k1": nrm(ks[5], (DEPTH, DIFF_HEAD_DIM), 0.1),
        "lambda_q2": nrm(ks[6], (DEPTH, DIFF_HEAD_DIM), 0.1),
        "lambda_k2": nrm(ks[7], (DEPTH, DIFF_HEAD_DIM), 0.1),
        "subln_g": gain(ks[8], (DEPTH, DIFF_V_DIM)),
        "gmlp_ln_g": gain(ks[9], (DEPTH, GMLP_WIDTH)),
        "gmlp_ln_b": nrm(ks[10], (DEPTH, GMLP_WIDTH), 0.01),
        "w_s": nrm(ks[11], (DEPTH, N_GMLP_GROUPS, CHUNK, CHUNK), CHUNK ** -0.5),
        "b_s": nrm(ks[12], (DEPTH, N_GMLP_GROUPS, CHUNK), 0.01),
        "w_out": nrm(ks[13], (DEPTH, MIX_WIDTH, D_MODEL), MIX_WIDTH ** -0.5),
        "post_mix_g": gain(ks[14], (DEPTH, D_MODEL)),
        "pre_ffn_g": gain(ks[15], (DEPTH, D_MODEL)),
        "w_gate": nrm(ks[16], (DEPTH, D_MODEL, D_FF), D_MODEL ** -0.5),
        "w_up": nrm(ks[17], (DEPTH, D_MODEL, D_FF), D_MODEL ** -0.5),
        "w_down": nrm(ks[18], (DEPTH, D_FF, D_MODEL), D_FF ** -0.5),
        "post_ffn_g": gain(ks[19], (DEPTH, D_MODEL)),
    }


def reference(x, positions, pre_mix_g, w_in, lambda_q1, lambda_k1, lambda_q2, lambda_k2,
              subln_g, gmlp_ln_g, gmlp_ln_b, w_s, b_s, w_out, post_mix_g,
              pre_ffn_g, w_gate, w_up, w_down, post_ffn_g):
    B, S = x.shape[0], x.shape[1]
    cos, sin = _rope_tables(positions)
    splits = [Q_WIDTH, Q_WIDTH + K_WIDTH, Q_WIDTH + K_WIDTH + V_WIDTH,
              Q_WIDTH + K_WIDTH + V_WIDTH + GMLP_WIDTH]
    for l in range(DEPTH):
        lambda_init = 0.8 - 0.6 * math.exp(-0.3 * l)
        h = _rmsnorm(x, pre_mix_g[l])
        proj = h @ w_in[l]
        q, k, v, gu, gv = jnp.split(proj, splits, axis=-1)
        q = _apply_rope(q.reshape(B, S, 2 * N_DIFF_HEADS, DIFF_HEAD_DIM), cos, sin)
        k = _apply_rope(k.reshape(B, S, 2 * N_DIFF_HEADS, DIFF_HEAD_DIM), cos, sin)
        v = v.reshape(B, S, N_DIFF_HEADS, DIFF_V_DIM)
        lam = (jnp.exp(jnp.sum(lambda_q1[l].astype(jnp.float32) * lambda_k1[l].astype(jnp.float32)))
               - jnp.exp(jnp.sum(lambda_q2[l].astype(jnp.float32) * lambda_k2[l].astype(jnp.float32)))
               + lambda_init)
        attn_out = _diff_attention(q, k, v, lam, lambda_init, subln_g[l])
        gmlp_out = _spatial_gating(jax.nn.gelu(gu, approximate=False),
                                   jax.nn.gelu(gv, approximate=False),
                                   gmlp_ln_g[l], gmlp_ln_b[l], w_s[l], b_s[l])
        mix = jnp.concatenate([attn_out, gmlp_out], axis=-1) @ w_out[l]
        x = x + _rmsnorm(mix, post_mix_g[l])
        h = _rmsnorm(x, pre_ffn_g[l])
        f = (jax.nn.silu(h @ w_gate[l]) * (h @ w_up[l])) @ w_down[l]
        x = x + _rmsnorm(f, post_ffn_g[l])
    return x
```

```python
import functools
import math

import jax
import jax.numpy as jnp
from jax import lax
from jax.experimental import pallas as pl
from jax.experimental.pallas import tpu as pltpu

N_DIFF_HEADS = 8
DIFF_HEAD_DIM = 64
DIFF_V_DIM = 2 * DIFF_HEAD_DIM
N_GMLP_GROUPS = 8
CHUNK = 128
ROPE_THETA = 10000.0
RMS_EPS = 1e-6
LN_EPS = 1e-5
SUBLN_EPS = 1e-5
LAMBDA_INIT = 0.8 - 0.6 * math.exp(-0.3 * 0)

LANES = 128
VMEM_LIMIT_BYTES = 56 * 1024 * 1024

F32 = jnp.float32
BF16 = jnp.bfloat16


def _rms(x, eps):
    return x * lax.rsqrt(jnp.mean(x * x, axis=-1, keepdims=True) + eps)


def _gelu(x):
    return 0.5 * x * (1.0 + lax.erf(x * (1.0 / math.sqrt(2.0))))


def _params(semantics):
    return pltpu.CompilerParams(dimension_semantics=semantics, vmem_limit_bytes=VMEM_LIMIT_BYTES)


def _rope_table_kernel(pos_ref, inv_ref, cos_ref, sin_ref):
    ang = pos_ref[...].astype(F32) * inv_ref[...]
    lane = lax.broadcasted_iota(jnp.int32, ang.shape, 1)
    first_half = (lane % DIFF_HEAD_DIM) < (DIFF_HEAD_DIM // 2)
    s = jnp.sin(ang)
    cos_ref[...] = jnp.cos(ang)
    sin_ref[...] = jnp.where(first_half, -s, s)


def _rope_tables(positions):
    seq = positions.shape[0]
    tm = 1024
    inv_freq = ROPE_THETA ** (-jnp.arange(0, DIFF_HEAD_DIM, 2, dtype=F32) / DIFF_HEAD_DIM)
    inv = jnp.tile(inv_freq, LANES // (DIFF_HEAD_DIM // 2))[None, :]
    return pl.pallas_call(
        _rope_table_kernel,
        grid=(seq // tm,),
        in_specs=[pl.BlockSpec((tm, 1), lambda i: (i, 0)), pl.BlockSpec((1, LANES), lambda i: (0, 0))],
        out_specs=[pl.BlockSpec((tm, LANES), lambda i: (i, 0))] * 2,
        out_shape=[jax.ShapeDtypeStruct((seq, LANES), F32)] * 2,
        compiler_params=_params(("arbitrary",)),
        name="rope_tables",
    )(positions[:, None], inv)


def _rope(x, cos, sin_signed):
    lane = lax.broadcasted_iota(jnp.int32, x.shape, 1)
    first_half = (lane % DIFF_HEAD_DIM) < (DIFF_HEAD_DIM // 2)
    half = DIFF_HEAD_DIM // 2
    partner = jnp.where(first_half, pltpu.roll(x, LANES - half, 1), pltpu.roll(x, half, 1))
    return x * cos + partner * sin_signed


def _inproj_kernel(x_ref, g_ref, w_ref, cos_ref, sin_ref, lng_ref, lnb_ref, ws_ref, bs_ref,
                   q_ref, k_ref, v_ref, gm_ref, h_ref, gu_ref, *, q_scale):
    j = pl.program_id(1)
    tm, width = q_ref.shape

    @pl.when(j == 0)
    def _():
        h_ref[...] = (_rms(x_ref[...], RMS_EPS) * g_ref[...]).astype(BF16)

    def proj():
        return jnp.dot(h_ref[...], w_ref[...], preferred_element_type=F32)

    def rope_store(out_ref, scale):
        acc = proj()
        cos = cos_ref[...]
        sin = sin_ref[...]
        for c in range(width // LANES):
            sl = slice(c * LANES, (c + 1) * LANES)
            out_ref[:, sl] = (_rope(acc[:, sl], cos, sin) * scale).astype(BF16)

    @pl.when(j == 0)
    def _():
        rope_store(q_ref, q_scale)

    @pl.when(j == 1)
    def _():
        rope_store(k_ref, 1.0)

    @pl.when(j == 2)
    def _():
        v_ref[...] = proj().astype(BF16)

    @pl.when(j == 3)
    def _():
        gu_ref[...] = _gelu(proj())

    @pl.when(j == 4)
    def _():
        gv = _gelu(proj())
        mu = jnp.mean(gv, axis=-1, keepdims=True)
        d = gv - mu
        var = jnp.mean(d * d, axis=-1, keepdims=True)
        vln = (d * lax.rsqrt(var + LN_EPS) * lng_ref[...] + lnb_ref[...]).astype(BF16)
        for c in range(tm // CHUNK):
            rows = slice(c * CHUNK, (c + 1) * CHUNK)
            for g in range(N_GMLP_GROUPS):
                cols = slice(g * LANES, (g + 1) * LANES)
                y = jnp.dot(ws_ref[g], vln[rows, cols], preferred_element_type=F32) + bs_ref[:, cols]
                gm_ref[rows, cols] = (gu_ref[rows, cols] * y).astype(BF16)


def _inproj(x, pre_g, w_in, cos, sin, ln_g, ln_b, w_s, bs_full):
    seq, d_model = x.shape
    width = 1024
    tm = 512
    n_sections = w_in.shape[1] // width
    row_spec = lambda cols: pl.BlockSpec((tm, cols), lambda i, j: (i, 0))
    const_spec = lambda shape: pl.BlockSpec(shape, lambda i, j: (0,) * len(shape))
    return pl.pallas_call(
        functools.partial(_inproj_kernel, q_scale=DIFF_HEAD_DIM ** -0.5),
        grid=(seq // tm, n_sections),
        in_specs=[
            row_spec(d_model),
            const_spec((1, d_model)),
            pl.BlockSpec((d_model, width), lambda i, j: (0, j)),
            row_spec(LANES),
            row_spec(LANES),
            const_spec((1, width)),
            const_spec((1, width)),
            const_spec(w_s.shape),
            const_spec(bs_full.shape),
        ],
        out_specs=[row_spec(width)] * 4,
        out_shape=[jax.ShapeDtypeStruct((seq, width), BF16)] * 4,
        scratch_shapes=[pltpu.VMEM((tm, d_model), BF16), pltpu.VMEM((tm, width), F32)],
        compiler_params=_params(("arbitrary", "arbitrary")),
        name="inproj",
    )(x, pre_g, w_in, cos, sin, ln_g, ln_b, w_s, bs_full)


def _attn_kernel(lq1_ref, lk1_ref, lq2_ref, lk2_ref, q_ref, k_ref, v_ref, g_ref, o_ref,
                 s_ref, p_ref, *, tkc):
    tq = q_ref.shape[0]
    nkc = k_ref.shape[0] // tkc
    lane = lax.broadcasted_iota(jnp.int32, (tq, LANES), 1)
    q = q_ref[...]
    nt_dims = (((1,), (1,)), ((), ()))

    def key_rows(kc):
        return pl.ds(pl.multiple_of(kc * tkc, tkc), tkc)

    def lane_fold(op, carry, tile):
        for t in range(tkc // LANES):
            carry = op(carry, tile[:, t * LANES:(t + 1) * LANES])
        return carry

    row_sums = []
    for mp in range(2):
        in_map = (lane < DIFF_HEAD_DIM) if mp == 0 else (lane >= DIFF_HEAD_DIM)
        qm = jnp.where(in_map, q, jnp.zeros_like(q))

        def scores(kc, mpart):
            s = lax.dot_general(qm, k_ref[key_rows(kc), :], nt_dims, preferred_element_type=F32)
            s_ref[kc] = s
            return lane_fold(jnp.maximum, mpart, s)

        mpart = lax.fori_loop(0, nkc, scores, jnp.full((tq, LANES), -jnp.inf, F32))
        m = jnp.max(mpart, axis=1, keepdims=True)

        def probs(kc, lpart):
            p = jnp.exp(s_ref[kc] - m)
            p_ref[mp, kc] = p.astype(BF16)
            return lane_fold(jnp.add, lpart, p)

        lpart = lax.fori_loop(0, nkc, probs, jnp.zeros((tq, LANES), F32))
        row_sums.append(jnp.sum(lpart, axis=1, keepdims=True))

    lam = (jnp.exp(jnp.sum(lq1_ref[...] * lk1_ref[...], axis=1, keepdims=True))
           - jnp.exp(jnp.sum(lq2_ref[...] * lk2_ref[...], axis=1, keepdims=True)) + LAMBDA_INIT)
    c1 = 1.0 / row_sums[0]
    c2 = lam / row_sums[1]

    def weighted_values(kc, acc):
        w = p_ref[0, kc].astype(F32) * c1 - p_ref[1, kc].astype(F32) * c2
        return acc + jnp.dot(w.astype(BF16), v_ref[key_rows(kc), :], preferred_element_type=F32)

    acc = lax.fori_loop(0, nkc, weighted_values, jnp.zeros((tq, LANES), F32))
    o_ref[...] = (_rms(acc, SUBLN_EPS) * g_ref[...] * (1.0 - LAMBDA_INIT)).astype(BF16)


def _attention(lq1, lk1, lq2, lk2, q, k, v, subln_g):
    seq = q.shape[0]
    tq = 512
    tkc = 512
    lam_spec = pl.BlockSpec((1, DIFF_HEAD_DIM), lambda h, i: (0, 0))
    return pl.pallas_call(
        functools.partial(_attn_kernel, tkc=tkc),
        grid=(N_DIFF_HEADS, seq // tq),
        in_specs=[
            lam_spec, lam_spec, lam_spec, lam_spec,
            pl.BlockSpec((tq, DIFF_V_DIM), lambda h, i: (i, h)),
            pl.BlockSpec((seq, DIFF_V_DIM), lambda h, i: (0, h)),
            pl.BlockSpec((seq, DIFF_V_DIM), lambda h, i: (0, h)),
            pl.BlockSpec((1, DIFF_V_DIM), lambda h, i: (0, 0)),
        ],
        out_specs=pl.BlockSpec((tq, DIFF_V_DIM), lambda h, i: (i, h)),
        out_shape=jax.ShapeDtypeStruct((seq, N_DIFF_HEADS * DIFF_V_DIM), BF16),
        scratch_shapes=[
            pltpu.VMEM((seq // tkc, tq, tkc), F32),
            pltpu.VMEM((2, seq // tkc, tq, tkc), BF16),
        ],
        compiler_params=_params(("arbitrary", "arbitrary")),
        name="diff_attention",
    )(lq1, lk1, lq2, lk2, q, k, v, subln_g)


def _outproj_kernel(a_ref, gm_ref, w_ref, x_ref, gpost_ref, gffn_ref, x1_ref, h2_ref):
    half = a_ref.shape[1]
    mix = (jnp.dot(a_ref[...], w_ref[0:half, :], preferred_element_type=F32)
           + jnp.dot(gm_ref[...], w_ref[half:2 * half, :], preferred_element_type=F32))
    x1 = x_ref[...] + _rms(mix, RMS_EPS) * gpost_ref[...]
    x1_ref[...] = x1
    h2_ref[...] = (_rms(x1, RMS_EPS) * gffn_ref[...]).astype(BF16)


def _outproj(attn, gm, w_out, x, post_mix_g, pre_ffn_g):
    seq, d_model = x.shape
    half = attn.shape[1]
    tm = 512
    row_spec = lambda cols: pl.BlockSpec((tm, cols), lambda i: (i, 0))
    const_spec = lambda shape: pl.BlockSpec(shape, lambda i: (0, 0))
    return pl.pallas_call(
        _outproj_kernel,
        grid=(seq // tm,),
        in_specs=[row_spec(half), row_spec(half), const_spec(w_out.shape), row_spec(d_model),
                  const_spec((1, d_model)), const_spec((1, d_model))],
        out_specs=[row_spec(d_model), row_spec(d_model)],
        out_shape=[jax.ShapeDtypeStruct((seq, d_model), F32), jax.ShapeDtypeStruct((seq, d_model), BF16)],
        compiler_params=_params(("arbitrary",)),
        name="outproj",
    )(attn, gm, w_out, x, post_mix_g, pre_ffn_g)


def _ffn_kernel(h_ref, wg_ref, wu_ref, wd_ref, x1_ref, g_ref, o_ref, acc_ref):
    j = pl.program_id(1)
    h = h_ref[...]
    gate = jnp.dot(h, wg_ref[...], preferred_element_type=F32)
    up = jnp.dot(h, wu_ref[...], preferred_element_type=F32)
    act = (gate * jax.nn.sigmoid(gate) * up).astype(BF16)
    part = jnp.dot(act, wd_ref[...], preferred_element_type=F32)

    @pl.when(j == 0)
    def _():
        acc_ref[...] = part

    @pl.when(j > 0)
    def _():
        acc_ref[...] += part

    @pl.when(j == pl.num_programs(1) - 1)
    def _():
        o_ref[...] = x1_ref[...] + _rms(acc_ref[...], RMS_EPS) * g_ref[...]


def _ffn(h2, w_gate, w_up, w_down, x1, post_ffn_g):
    seq, d_model = x1.shape
    d_ff = w_gate.shape[1]
    tm = 512
    tf = 512
    return pl.pallas_call(
        _ffn_kernel,
        grid=(seq // tm, d_ff // tf),
        in_specs=[
            pl.BlockSpec((tm, d_model), lambda i, j: (i, 0)),
            pl.BlockSpec((d_model, tf), lambda i, j: (0, j)),
            pl.BlockSpec((d_model, tf), lambda i, j: (0, j)),
            pl.BlockSpec((tf, d_model), lambda i, j: (j, 0)),
            pl.BlockSpec((tm, d_model), lambda i, j: (i, 0)),
            pl.BlockSpec((1, d_model), lambda i, j: (0, 0)),
        ],
        out_specs=pl.BlockSpec((tm, d_model), lambda i, j: (i, 0)),
        out_shape=jax.ShapeDtypeStruct((seq, d_model), F32),
        scratch_shapes=[pltpu.VMEM((tm, d_model), F32)],
        compiler_params=_params(("arbitrary", "arbitrary")),
        name="ffn",
    )(h2, w_gate, w_up, w_down, x1, post_ffn_g)


def kernel(x, positions, pre_mix_g, w_in, lambda_q1, lambda_k1, lambda_q2, lambda_k2, subln_g,
           gmlp_ln_g, gmlp_ln_b, w_s, b_s, w_out, post_mix_g, pre_ffn_g, w_gate, w_up, w_down,
           post_ffn_g):
    batch, seq, d_model = x.shape
    assert batch == 1 and pre_mix_g.shape[0] == 1, "single sequence, single layer"
    assert seq % 1024 == 0 and d_model == 2048
    x2 = x[0]
    cos, sin = _rope_tables(positions[0])
    bs_full = jnp.repeat(b_s[0].T, LANES, axis=1)
    q, k, v, gm = _inproj(x2, pre_mix_g, w_in[0].astype(BF16), cos, sin, gmlp_ln_g, gmlp_ln_b,
                          w_s[0].astype(BF16), bs_full)
    attn = _attention(lambda_q1, lambda_k1, lambda_q2, lambda_k2, q, k, v, subln_g)
    x1, h2 = _outproj(attn, gm, w_out[0].astype(BF16), x2, post_mix_g, pre_ffn_g)
    out = _ffn(h2, w_gate[0].astype(BF16), w_up[0].astype(BF16), w_down[0].astype(BF16), x1,
               post_ffn_g)
    return out[None]
```

```python
import functools
import math

import jax
import jax.numpy as jnp
from jax import lax
from jax.experimental import pallas as pl
from jax.experimental.pallas import tpu as pltpu

N_DIFF_HEADS = 8
DIFF_HEAD_DIM = 64
DIFF_V_DIM = 2 * DIFF_HEAD_DIM
N_GMLP_GROUPS = 8
CHUNK = 128
ROPE_THETA = 10000.0
RMS_EPS = 1e-6
LN_EPS = 1e-5
SUBLN_EPS = 1e-5
LAMBDA_INIT = 0.8 - 0.6 * math.exp(-0.3 * 0)

LANES = 128
VMEM_LIMIT_BYTES = 56 * 1024 * 1024

F32 = jnp.float32
BF16 = jnp.bfloat16


def _rms(x, eps):
    return x * lax.rsqrt(jnp.mean(x * x, axis=-1, keepdims=True) + eps)


def _gelu(x):
    return 0.5 * x * (1.0 + lax.erf(x * (1.0 / math.sqrt(2.0))))


def _params(semantics):
    return pltpu.CompilerParams(dimension_semantics=semantics, vmem_limit_bytes=VMEM_LIMIT_BYTES)


def _rope_table_kernel(pos_ref, inv_ref, cos_ref, sin_ref):
    ang = pos_ref[...].astype(F32) * inv_ref[...]
    lane = lax.broadcasted_iota(jnp.int32, ang.shape, 1)
    first_half = (lane % DIFF_HEAD_DIM) < (DIFF_HEAD_DIM // 2)
    s = jnp.sin(ang)
    cos_ref[...] = jnp.cos(ang)
    sin_ref[...] = jnp.where(first_half, -s, s)


def _rope_tables(positions):
    seq = positions.shape[0]
    tm = 1024
    inv_freq = ROPE_THETA ** (-jnp.arange(0, DIFF_HEAD_DIM, 2, dtype=F32) / DIFF_HEAD_DIM)
    inv = jnp.tile(inv_freq, LANES // (DIFF_HEAD_DIM // 2))[None, :]
    return pl.pallas_call(
        _rope_table_kernel,
        grid=(seq // tm,),
        in_specs=[pl.BlockSpec((tm, 1), lambda i: (i, 0)), pl.BlockSpec((1, LANES), lambda i: (0, 0))],
        out_specs=[pl.BlockSpec((tm, LANES), lambda i: (i, 0))] * 2,
        out_shape=[jax.ShapeDtypeStruct((seq, LANES), F32)] * 2,
        compiler_params=_params(("arbitrary",)),
        name="rope_tables",
    )(positions[:, None], inv)


def _rope(x, cos, sin_signed):
    lane = lax.broadcasted_iota(jnp.int32, x.shape, 1)
    first_half = (lane % DIFF_HEAD_DIM) < (DIFF_HEAD_DIM // 2)
    half = DIFF_HEAD_DIM // 2
    partner = jnp.where(first_half, pltpu.roll(x, LANES - half, 1), pltpu.roll(x, half, 1))
    return x * cos + partner * sin_signed


def _inproj_kernel(x_ref, g_ref, w_ref, cos_ref, sin_ref, lng_ref, lnb_ref, ws_ref, bs_ref,
                   q_ref, k_ref, v_ref, gm_ref, h_ref, gu_ref, *, q_scale):
    j = pl.program_id(1)
    tm, width = q_ref.shape

    @pl.when(j == 0)
    def _():
        h_ref[...] = (_rms(x_ref[...], RMS_EPS) * g_ref[...]).astype(BF16)

    def proj():
        return jnp.dot(h_ref[...], w_ref[...], preferred_element_type=F32)

    def rope_store(out_ref, scale):
        acc = proj()
        cos = cos_ref[...]
        sin = sin_ref[...]
        for c in range(width // LANES):
            sl = slice(c * LANES, (c + 1) * LANES)
            out_ref[:, sl] = (_rope(acc[:, sl], cos, sin) * scale).astype(BF16)

    @pl.when(j == 0)
    def _():
        rope_store(q_ref, q_scale)

    @pl.when(j == 1)
    def _():
        rope_store(k_ref, 1.0)

    @pl.when(j == 2)
    def _():
        v_ref[...] = proj().astype(BF16)

    @pl.when(j == 3)
    def _():
        gu_ref[...] = _gelu(proj())

    @pl.when(j == 4)
    def _():
        gv = _gelu(proj())
        mu = jnp.mean(gv, axis=-1, keepdims=True)
        d = gv - mu
        var = jnp.mean(d * d, axis=-1, keepdims=True)
        vln = (d * lax.rsqrt(var + LN_EPS) * lng_ref[...] + lnb_ref[...]).astype(BF16)
        for c in range(tm // CHUNK):
            rows = slice(c * CHUNK, (c + 1) * CHUNK)
            for g in range(N_GMLP_GROUPS):
                cols = slice(g * LANES, (g + 1) * LANES)
                y = jnp.dot(ws_ref[g], vln[rows, cols], preferred_element_type=F32) + bs_ref[:, cols]
                gm_ref[rows, cols] = (gu_ref[rows, cols] * y).astype(BF16)


def _inproj(x, pre_g, w_in, cos, sin, ln_g, ln_b, w_s, bs_full):
    seq, d_model = x.shape
    width = 1024
    tm = 512
    n_sections = w_in.shape[1] // width
    row_spec = lambda cols: pl.BlockSpec((tm, cols), lambda i, j: (i, 0))
    const_spec = lambda shape: pl.BlockSpec(shape, lambda i, j: (0,) * len(shape))
    return pl.pallas_call(
        functools.partial(_inproj_kernel, q_scale=DIFF_HEAD_DIM ** -0.5 * math.log2(math.e)),
        grid=(seq // tm, n_sections),
        in_specs=[
            row_spec(d_model),
            const_spec((1, d_model)),
            pl.BlockSpec((d_model, width), lambda i, j: (0, j)),
            row_spec(LANES),
            row_spec(LANES),
            const_spec((1, width)),
            const_spec((1, width)),
            const_spec(w_s.shape),
            const_spec(bs_full.shape),
        ],
        out_specs=[row_spec(width)] * 4,
        out_shape=[jax.ShapeDtypeStruct((seq, width), BF16)] * 4,
        scratch_shapes=[pltpu.VMEM((tm, d_model), BF16), pltpu.VMEM((tm, width), F32)],
        compiler_params=_params(("arbitrary", "arbitrary")),
        name="inproj",
    )(x, pre_g, w_in, cos, sin, ln_g, ln_b, w_s, bs_full)


def _attn_kernel(lq1_ref, lk1_ref, lq2_ref, lk2_ref, q_ref, k_ref, v_ref, g_ref, o_ref,
                 s_ref, p_ref, mpart_ref, mfull_ref, lpart_ref, acc_ref, *, tkc, row_block, unroll):
    tq = q_ref.shape[0]
    nkc = k_ref.shape[0] // tkc
    lane = lax.broadcasted_iota(jnp.int32, (tq, LANES), 1)
    q = q_ref[...]
    zero = jnp.zeros_like(q)
    q1 = jnp.where(lane < DIFF_HEAD_DIM, q, zero)
    q2 = jnp.where(lane >= DIFF_HEAD_DIM, q, zero)
    nt_dims = (((1,), (1,)), ((), ()))

    def key_rows(kc):
        return pl.ds(pl.multiple_of(kc * tkc, tkc), tkc)

    row_blocks = [slice(r, r + row_block) for r in range(0, tq, row_block)]
    lane_tiles = [slice(t, t + LANES) for t in range(0, tkc, LANES)]

    def scores(q_map, kc):
        s = lax.dot_general(q_map, k_ref[key_rows(kc), :], nt_dims, preferred_element_type=F32)
        s_ref[kc] = s
        for rows in row_blocks:
            m = mpart_ref[rows]
            for t in lane_tiles:
                m = jnp.maximum(m, s[rows, t])
            mpart_ref[rows] = m

    def probs(mp, kc):
        for rows in row_blocks:
            m = mfull_ref[rows]
            l = lpart_ref[mp, rows]
            for t in lane_tiles:
                p = jnp.exp2(s_ref[kc, rows, t] - m)
                p_ref[mp, kc, rows, t] = p.astype(BF16)
                l = l + p
            lpart_ref[mp, rows] = l

    def weighted_values(mp, kc):
        acc_ref[mp] += jnp.dot(p_ref[mp, kc], v_ref[key_rows(kc), :], preferred_element_type=F32)

    def finish_max():
        mfull_ref[...] = jnp.broadcast_to(jnp.max(mpart_ref[...], axis=1, keepdims=True), (tq, LANES))
        mpart_ref[...] = jnp.full((tq, LANES), -jnp.inf, F32)

    def sweep(body):
        lax.fori_loop(0, nkc, lambda kc, c: (body(kc), c)[1], 0, unroll=unroll)

    mpart_ref[...] = jnp.full((tq, LANES), -jnp.inf, F32)
    lpart_ref[...] = jnp.zeros_like(lpart_ref)
    acc_ref[...] = jnp.zeros_like(acc_ref)

    sweep(lambda kc: scores(q1, kc))
    finish_max()

    def probs1_scores2(kc):
        probs(0, kc)
        scores(q2, kc)

    sweep(probs1_scores2)
    finish_max()

    def probs2_values1(kc):
        probs(1, kc)
        weighted_values(0, kc)

    sweep(probs2_values1)
    sweep(lambda kc: weighted_values(1, kc))

    lam = (jnp.exp(jnp.sum(lq1_ref[...] * lk1_ref[...], axis=1, keepdims=True))
           - jnp.exp(jnp.sum(lq2_ref[...] * lk2_ref[...], axis=1, keepdims=True)) + LAMBDA_INIT)
    l1 = jnp.sum(lpart_ref[0], axis=1, keepdims=True)
    l2 = jnp.sum(lpart_ref[1], axis=1, keepdims=True)
    out = acc_ref[0] * (1.0 / l1) - acc_ref[1] * (lam / l2)
    o_ref[...] = (_rms(out, SUBLN_EPS) * g_ref[...] * (1.0 - LAMBDA_INIT)).astype(BF16)


def _attention(lq1, lk1, lq2, lk2, q, k, v, subln_g):
    seq = q.shape[0]
    tq = 512
    tkc = 1024
    lam_spec = pl.BlockSpec((1, DIFF_HEAD_DIM), lambda h, i: (0, 0))
    return pl.pallas_call(
        functools.partial(_attn_kernel, tkc=tkc, row_block=64, unroll=2),
        grid=(N_DIFF_HEADS, seq // tq),
        in_specs=[
            lam_spec, lam_spec, lam_spec, lam_spec,
            pl.BlockSpec((tq, DIFF_V_DIM), lambda h, i: (i, h)),
            pl.BlockSpec((seq, DIFF_V_DIM), lambda h, i: (0, h)),
            pl.BlockSpec((seq, DIFF_V_DIM), lambda h, i: (0, h)),
            pl.BlockSpec((1, DIFF_V_DIM), lambda h, i: (0, 0)),
        ],
        out_specs=pl.BlockSpec((tq, DIFF_V_DIM), lambda h, i: (i, h)),
        out_shape=jax.ShapeDtypeStruct((seq, N_DIFF_HEADS * DIFF_V_DIM), BF16),
        scratch_shapes=[pltpu.VMEM((seq // tkc, tq, tkc), F32),
                        pltpu.VMEM((2, seq // tkc, tq, tkc), BF16),
                        pltpu.VMEM((tq, LANES), F32),
                        pltpu.VMEM((tq, LANES), F32),
                        pltpu.VMEM((2, tq, LANES), F32),
                        pltpu.VMEM((2, tq, LANES), F32)],
        compiler_params=_params(("arbitrary", "arbitrary")),
        name="diff_attention",
    )(lq1, lk1, lq2, lk2, q, k, v, subln_g)


def _outproj_kernel(a_ref, gm_ref, w_ref, x_ref, gpost_ref, gffn_ref, x1_ref, h2_ref):
    half = a_ref.shape[1]
    mix = (jnp.dot(a_ref[...], w_ref[0:half, :], preferred_element_type=F32)
           + jnp.dot(gm_ref[...], w_ref[half:2 * half, :], preferred_element_type=F32))
    x1 = x_ref[...] + _rms(mix, RMS_EPS) * gpost_ref[...]
    x1_ref[...] = x1
    h2_ref[...] = (_rms(x1, RMS_EPS) * gffn_ref[...]).astype(BF16)


def _outproj(attn, gm, w_out, x, post_mix_g, pre_ffn_g):
    seq, d_model = x.shape
    half = attn.shape[1]
    tm = 512
    row_spec = lambda cols: pl.BlockSpec((tm, cols), lambda i: (i, 0))
    const_spec = lambda shape: pl.BlockSpec(shape, lambda i: (0, 0))
    return pl.pallas_call(
        _outproj_kernel,
        grid=(seq // tm,),
        in_specs=[row_spec(half), row_spec(half), const_spec(w_out.shape), row_spec(d_model),
                  const_spec((1, d_model)), const_spec((1, d_model))],
        out_specs=[row_spec(d_model), row_spec(d_model)],
        out_shape=[jax.ShapeDtypeStruct((seq, d_model), F32), jax.ShapeDtypeStruct((seq, d_model), BF16)],
        compiler_params=_params(("arbitrary",)),
        name="outproj",
    )(attn, gm, w_out, x, post_mix_g, pre_ffn_g)


def _ffn_kernel(h_ref, wg_ref, wu_ref, wd_ref, x1_ref, g_ref, o_ref, acc_ref):
    j = pl.program_id(1)
    h = h_ref[...]
    gate = jnp.dot(h, wg_ref[...], preferred_element_type=F32)
    up = jnp.dot(h, wu_ref[...], preferred_element_type=F32)
    act = (gate * jax.nn.sigmoid(gate) * up).astype(BF16)
    part = jnp.dot(act, wd_ref[...], preferred_element_type=F32)

    @pl.when(j == 0)
    def _():
        acc_ref[...] = part

    @pl.when(j > 0)
    def _():
        acc_ref[...] += part

    @pl.when(j == pl.num_programs(1) - 1)
    def _():
        o_ref[...] = x1_ref[...] + _rms(acc_ref[...], RMS_EPS) * g_ref[...]


def _ffn(h2, w_gate, w_up, w_down, x1, post_ffn_g):
    seq, d_model = x1.shape
    d_ff = w_gate.shape[1]
    tm = 512
    tf = 512
    return pl.pallas_call(
        _ffn_kernel,
        grid=(seq // tm, d_ff // tf),
        in_specs=[
            pl.BlockSpec((tm, d_model), lambda i, j: (i, 0)),
            pl.BlockSpec((d_model, tf), lambda i, j: (0, j)),
            pl.BlockSpec((d_model, tf), lambda i, j: (0, j)),
            pl.BlockSpec((tf, d_model), lambda i, j: (j, 0)),
            pl.BlockSpec((tm, d_model), lambda i, j: (i, 0)),
            pl.BlockSpec((1, d_model), lambda i, j: (0, 0)),
        ],
        out_specs=pl.BlockSpec((tm, d_model), lambda i, j: (i, 0)),
        out_shape=jax.ShapeDtypeStruct((seq, d_model), F32),
        scratch_shapes=[pltpu.VMEM((tm, d_model), F32)],
        compiler_params=_params(("arbitrary", "arbitrary")),
        name="ffn",
    )(h2, w_gate, w_up, w_down, x1, post_ffn_g)


def kernel(x, positions, pre_mix_g, w_in, lambda_q1, lambda_k1, lambda_q2, lambda_k2, subln_g,
           gmlp_ln_g, gmlp_ln_b, w_s, b_s, w_out, post_mix_g, pre_ffn_g, w_gate, w_up, w_down,
           post_ffn_g):
    batch, seq, d_model = x.shape
    assert batch == 1 and pre_mix_g.shape[0] == 1, "single sequence, single layer"
    assert seq % 1024 == 0 and d_model == 2048
    x2 = x[0]
    cos, sin = _rope_tables(positions[0])
    bs_full = jnp.repeat(b_s[0].T, LANES, axis=1)
    q, k, v, gm = _inproj(x2, pre_mix_g, w_in[0].astype(BF16), cos, sin, gmlp_ln_g, gmlp_ln_b,
                          w_s[0].astype(BF16), bs_full)
    attn = _attention(lambda_q1, lambda_k1, lambda_q2, lambda_k2, q, k, v, subln_g)
    x1, h2 = _outproj(attn, gm, w_out[0].astype(BF16), x2, post_mix_g, pre_ffn_g)
    out = _ffn(h2, w_gate[0].astype(BF16), w_up[0].astype(BF16), w_down[0].astype(BF16), x1,
               post_ffn_g)
    return out[None]
```

```python
import functools
import math

import jax
import jax.numpy as jnp
from jax import lax
from jax.experimental import pallas as pl
from jax.experimental.pallas import tpu as pltpu

N_DIFF_HEADS = 8
DIFF_HEAD_DIM = 64
DIFF_V_DIM = 2 * DIFF_HEAD_DIM
N_GMLP_GROUPS = 8
CHUNK = 128
ROPE_THETA = 10000.0
RMS_EPS = 1e-6
LN_EPS = 1e-5
SUBLN_EPS = 1e-5
LAMBDA_INIT = 0.8 - 0.6 * math.exp(-0.3 * 0)

LANES = 128
VMEM_LIMIT_BYTES = 56 * 1024 * 1024

F32 = jnp.float32
BF16 = jnp.bfloat16


def _rms(x, eps):
    return x * lax.rsqrt(jnp.mean(x * x, axis=-1, keepdims=True) + eps)


def _gelu(x):
    return 0.5 * x * (1.0 + lax.erf(x * (1.0 / math.sqrt(2.0))))


def _params(semantics):
    return pltpu.CompilerParams(dimension_semantics=semantics, vmem_limit_bytes=VMEM_LIMIT_BYTES)


def _rope_table_kernel(pos_ref, inv_ref, cos_ref, sin_ref):
    ang = pos_ref[...].astype(F32) * inv_ref[...]
    lane = lax.broadcasted_iota(jnp.int32, ang.shape, 1)
    first_half = (lane % DIFF_HEAD_DIM) < (DIFF_HEAD_DIM // 2)
    s = jnp.sin(ang)
    cos_ref[...] = jnp.cos(ang)
    sin_ref[...] = jnp.where(first_half, -s, s)


def _rope_tables(positions):
    seq = positions.shape[0]
    tm = 1024
    inv_freq = ROPE_THETA ** (-jnp.arange(0, DIFF_HEAD_DIM, 2, dtype=F32) / DIFF_HEAD_DIM)
    inv = jnp.tile(inv_freq, LANES // (DIFF_HEAD_DIM // 2))[None, :]
    return pl.pallas_call(
        _rope_table_kernel,
        grid=(seq // tm,),
        in_specs=[pl.BlockSpec((tm, 1), lambda i: (i, 0)), pl.BlockSpec((1, LANES), lambda i: (0, 0))],
        out_specs=[pl.BlockSpec((tm, LANES), lambda i: (i, 0))] * 2,
        out_shape=[jax.ShapeDtypeStruct((seq, LANES), F32)] * 2,
        compiler_params=_params(("arbitrary",)),
        name="rope_tables",
    )(positions[:, None], inv)


def _rope(x, cos, sin_signed):
    lane = lax.broadcasted_iota(jnp.int32, x.shape, 1)
    first_half = (lane % DIFF_HEAD_DIM) < (DIFF_HEAD_DIM // 2)
    half = DIFF_HEAD_DIM // 2
    partner = jnp.where(first_half, pltpu.roll(x, LANES - half, 1), pltpu.roll(x, half, 1))
    return x * cos + partner * sin_signed


def _inproj_kernel(x_ref, g_ref, w_ref, cos_ref, sin_ref, lng_ref, lnb_ref, ws_ref, bs_ref,
                   q_ref, k_ref, v_ref, gm_ref, h_ref, gu_ref, *, q_scale):
    j = pl.program_id(1)
    tm, width = q_ref.shape

    @pl.when(j == 0)
    def _():
        h_ref[...] = (_rms(x_ref[...], RMS_EPS) * g_ref[...]).astype(BF16)

    def proj():
        return jnp.dot(h_ref[...], w_ref[...], preferred_element_type=F32)

    def rope_store(out_ref, scale):
        acc = proj()
        cos = cos_ref[...]
        sin = sin_ref[...]
        for c in range(width // LANES):
            sl = slice(c * LANES, (c + 1) * LANES)
            out_ref[:, sl] = (_rope(acc[:, sl], cos, sin) * scale).astype(BF16)

    @pl.when(j == 0)
    def _():
        rope_store(q_ref, q_scale)

    @pl.when(j == 1)
    def _():
        rope_store(k_ref, 1.0)

    @pl.when(j == 2)
    def _():
        v_ref[...] = proj().astype(BF16)

    @pl.when(j == 3)
    def _():
        gu_ref[...] = _gelu(proj())

    @pl.when(j == 4)
    def _():
        gv = _gelu(proj())
        mu = jnp.mean(gv, axis=-1, keepdims=True)
        d = gv - mu
        var = jnp.mean(d * d, axis=-1, keepdims=True)
        vln = (d * lax.rsqrt(var + LN_EPS) * lng_ref[...] + lnb_ref[...]).astype(BF16)
        for c in range(tm // CHUNK):
            rows = slice(c * CHUNK, (c + 1) * CHUNK)
            for g in range(N_GMLP_GROUPS):
                cols = slice(g * LANES, (g + 1) * LANES)
                y = jnp.dot(ws_ref[g], vln[rows, cols], preferred_element_type=F32) + bs_ref[:, cols]
                gm_ref[rows, cols] = (gu_ref[rows, cols] * y).astype(BF16)


def _inproj(x, pre_g, w_in, cos, sin, ln_g, ln_b, w_s, bs_full):
    seq, d_model = x.shape
    width = 1024
    tm = 512
    n_sections = w_in.shape[1] // width
    row_spec = lambda cols: pl.BlockSpec((tm, cols), lambda i, j: (i, 0))
    const_spec = lambda shape: pl.BlockSpec(shape, lambda i, j: (0,) * len(shape))
    return pl.pallas_call(
        functools.partial(_inproj_kernel, q_scale=DIFF_HEAD_DIM ** -0.5 * math.log2(math.e)),
        grid=(seq // tm, n_sections),
        in_specs=[
            row_spec(d_model),
            const_spec((1, d_model)),
            pl.BlockSpec((d_model, width), lambda i, j: (0, j)),
            row_spec(LANES),
            row_spec(LANES),
            const_spec((1, width)),
            const_spec((1, width)),
            const_spec(w_s.shape),
            const_spec(bs_full.shape),
        ],
        out_specs=[row_spec(width)] * 4,
        out_shape=[jax.ShapeDtypeStruct((seq, width), BF16)] * 4,
        scratch_shapes=[pltpu.VMEM((tm, d_model), BF16), pltpu.VMEM((tm, width), F32)],
        compiler_params=_params(("arbitrary", "arbitrary")),
        name="inproj",
    )(x, pre_g, w_in, cos, sin, ln_g, ln_b, w_s, bs_full)


def _attn_kernel(lq1_ref, lk1_ref, lq2_ref, lk2_ref, q_ref, k_ref, v_ref, g_ref, o_ref,
                 s_ref, p_ref, mpart_ref, mfull_ref, lpart_ref, acc_ref, knorm_ref,
                 *, tkc, row_block, unroll):
    tq = q_ref.shape[0]
    nkc = k_ref.shape[0] // tkc
    lane = lax.broadcasted_iota(jnp.int32, (tq, LANES), 1)
    in_map1 = lane < DIFF_HEAD_DIM
    q = q_ref[...]
    zero = jnp.zeros_like(q)
    q1 = jnp.where(in_map1, q, zero)
    q2 = jnp.where(in_map1, zero, q)
    nt_dims = (((1,), (1,)), ((), ()))

    def key_rows(kc):
        return pl.ds(pl.multiple_of(kc * tkc, tkc), tkc)

    row_blocks = [slice(r, r + row_block) for r in range(0, tq, row_block)]
    lane_tiles = [slice(t, t + LANES) for t in range(0, tkc, LANES)]

    def sweep(body, unroll):
        lax.fori_loop(0, nkc, lambda kc, c: (body(kc), c)[1], 0, unroll=unroll)

    def map_sq_norms(x):
        sq = x.astype(F32)
        sq = sq * sq
        return (jnp.sum(jnp.where(in_map1[:x.shape[0]], sq, 0.0), axis=1, keepdims=True),
                jnp.sum(jnp.where(in_map1[:x.shape[0]], 0.0, sq), axis=1, keepdims=True))

    @pl.when(pl.program_id(1) == 0)
    def _():
        def chunk_max(kc, carry):
            n1, n2 = map_sq_norms(k_ref[pl.ds(pl.multiple_of(kc * tq, tq), tq), :])
            return (jnp.maximum(carry[0], jnp.max(n1, axis=0, keepdims=True)),
                    jnp.maximum(carry[1], jnp.max(n2, axis=0, keepdims=True)))

        zero11 = jnp.zeros((1, 1), F32)
        k1, k2 = lax.fori_loop(0, k_ref.shape[0] // tq, chunk_max, (zero11, zero11))
        knorm_ref[0] = jnp.broadcast_to(jnp.sqrt(k1), (8, LANES))
        knorm_ref[1] = jnp.broadcast_to(jnp.sqrt(k2), (8, LANES))

    lam = (jnp.exp(jnp.sum(lq1_ref[...] * lk1_ref[...], axis=1, keepdims=True))
           - jnp.exp(jnp.sum(lq2_ref[...] * lk2_ref[...], axis=1, keepdims=True)) + LAMBDA_INIT)

    def emit(out):
        o_ref[...] = (_rms(out, SUBLN_EPS) * g_ref[...] * (1.0 - LAMBDA_INIT)).astype(BF16)

    def row_sums():
        return (jnp.sum(lpart_ref[0], axis=1, keepdims=True),
                jnp.sum(lpart_ref[1], axis=1, keepdims=True))

    qn1, qn2 = map_sq_norms(q)
    mfull_ref[0] = jnp.sqrt(qn1) * knorm_ref[0, 0:1, :]
    mfull_ref[1] = jnp.sqrt(qn2) * knorm_ref[1, 0:1, :]
    lpart_ref[...] = jnp.zeros_like(lpart_ref)
    q_maps = jnp.concatenate([q1, q2], axis=0)

    def bounded_probs(kc):
        s = lax.dot_general(q_maps, k_ref[key_rows(kc), :], nt_dims, preferred_element_type=F32)
        for mp in range(2):
            for rows in row_blocks:
                shift = mfull_ref[mp, rows]
                l = lpart_ref[mp, rows]
                for t in lane_tiles:
                    p = jnp.exp2(s[mp * tq + rows.start:mp * tq + rows.stop, t] - shift)
                    p_ref[mp, kc, rows, t] = p.astype(BF16)
                    l = l + p
                lpart_ref[mp, rows] = l

    sweep(bounded_probs, unroll)
    l1, l2 = row_sums()
    smallest = jnp.minimum(jnp.min(l1), jnp.min(l2))
    fast_ok = smallest >= 2.0 ** -60

    def fast_finish():
        ratio = jnp.broadcast_to(lam * l1 / l2, (tq, LANES)).astype(BF16)
        acc_ref[0] = jnp.zeros((tq, LANES), F32)

        def combined_values(kc):
            w = jnp.concatenate([p_ref[0, kc, :, t] - p_ref[1, kc, :, t] * ratio for t in lane_tiles],
                                axis=1)
            acc_ref[0] += jnp.dot(w, v_ref[key_rows(kc), :], preferred_element_type=F32)

        sweep(combined_values, unroll)
        emit(acc_ref[0] * (1.0 / l1))

    def scores(q_map, kc):
        s = lax.dot_general(q_map, k_ref[key_rows(kc), :], nt_dims, preferred_element_type=F32)
        s_ref[kc] = s
        for rows in row_blocks:
            m = mpart_ref[rows]
            for t in lane_tiles:
                m = jnp.maximum(m, s[rows, t])
            mpart_ref[rows] = m

    def probs(mp, kc):
        for rows in row_blocks:
            m = mfull_ref[0, rows]
            l = lpart_ref[mp, rows]
            for t in lane_tiles:
                p = jnp.exp2(s_ref[kc, rows, t] - m)
                p_ref[mp, kc, rows, t] = p.astype(BF16)
                l = l + p
            lpart_ref[mp, rows] = l

    def weighted_values(mp, kc):
        acc_ref[mp] += jnp.dot(p_ref[mp, kc], v_ref[key_rows(kc), :], preferred_element_type=F32)

    def finish_max():
        mfull_ref[0] = jnp.broadcast_to(jnp.max(mpart_ref[...], axis=1, keepdims=True), (tq, LANES))
        mpart_ref[...] = jnp.full((tq, LANES), -jnp.inf, F32)

    def exact_path():
        mpart_ref[...] = jnp.full((tq, LANES), -jnp.inf, F32)
        lpart_ref[...] = jnp.zeros_like(lpart_ref)
        acc_ref[...] = jnp.zeros_like(acc_ref)

        sweep(lambda kc: scores(q1, kc), 1)
        finish_max()

        def probs1_scores2(kc):
            probs(0, kc)
            scores(q2, kc)

        sweep(probs1_scores2, 1)
        finish_max()

        def probs2_values1(kc):
            probs(1, kc)
            weighted_values(0, kc)

        sweep(probs2_values1, 1)
        sweep(lambda kc: weighted_values(1, kc), 1)
        e1, e2 = row_sums()
        emit(acc_ref[0] * (1.0 / e1) - acc_ref[1] * (lam / e2))

    lax.cond(fast_ok, fast_finish, exact_path)


def _attention(lq1, lk1, lq2, lk2, q, k, v, subln_g):
    seq = q.shape[0]
    tq = 512
    tkc = 1024
    lam_spec = pl.BlockSpec((1, DIFF_HEAD_DIM), lambda h, i: (0, 0))
    return pl.pallas_call(
        functools.partial(_attn_kernel, tkc=tkc, row_block=64, unroll=2),
        grid=(N_DIFF_HEADS, seq // tq),
        in_specs=[
            lam_spec, lam_spec, lam_spec, lam_spec,
            pl.BlockSpec((tq, DIFF_V_DIM), lambda h, i: (i, h)),
            pl.BlockSpec((seq, DIFF_V_DIM), lambda h, i: (0, h)),
            pl.BlockSpec((seq, DIFF_V_DIM), lambda h, i: (0, h)),
            pl.BlockSpec((1, DIFF_V_DIM), lambda h, i: (0, 0)),
        ],
        out_specs=pl.BlockSpec((tq, DIFF_V_DIM), lambda h, i: (i, h)),
        out_shape=jax.ShapeDtypeStruct((seq, N_DIFF_HEADS * DIFF_V_DIM), BF16),
        scratch_shapes=[pltpu.VMEM((seq // tkc, tq, tkc), F32),
                        pltpu.VMEM((2, seq // tkc, tq, tkc), BF16),
                        pltpu.VMEM((tq, LANES), F32),
                        pltpu.VMEM((2, tq, LANES), F32),
                        pltpu.VMEM((2, tq, LANES), F32),
                        pltpu.VMEM((2, tq, LANES), F32),
                        pltpu.VMEM((2, 8, LANES), F32)],
        compiler_params=_params(("arbitrary", "arbitrary")),
        name="diff_attention",
    )(lq1, lk1, lq2, lk2, q, k, v, subln_g)


def _outproj_kernel(a_ref, gm_ref, w_ref, x_ref, gpost_ref, gffn_ref, x1_ref, h2_ref):
    half = a_ref.shape[1]
    mix = (jnp.dot(a_ref[...], w_ref[0:half, :], preferred_element_type=F32)
           + jnp.dot(gm_ref[...], w_ref[half:2 * half, :], preferred_element_type=F32))
    x1 = x_ref[...] + _rms(mix, RMS_EPS) * gpost_ref[...]
    x1_ref[...] = x1
    h2_ref[...] = (_rms(x1, RMS_EPS) * gffn_ref[...]).astype(BF16)


def _outproj(attn, gm, w_out, x, post_mix_g, pre_ffn_g):
    seq, d_model = x.shape
    half = attn.shape[1]
    tm = 512
    row_spec = lambda cols: pl.BlockSpec((tm, cols), lambda i: (i, 0))
    const_spec = lambda shape: pl.BlockSpec(shape, lambda i: (0, 0))
    return pl.pallas_call(
        _outproj_kernel,
        grid=(seq // tm,),
        in_specs=[row_spec(half), row_spec(half), const_spec(w_out.shape), row_spec(d_model),
                  const_spec((1, d_model)), const_spec((1, d_model))],
        out_specs=[row_spec(d_model), row_spec(d_model)],
        out_shape=[jax.ShapeDtypeStruct((seq, d_model), F32), jax.ShapeDtypeStruct((seq, d_model), BF16)],
        compiler_params=_params(("arbitrary",)),
        name="outproj",
    )(attn, gm, w_out, x, post_mix_g, pre_ffn_g)


def _ffn_kernel(h_ref, wg_ref, wu_ref, wd_ref, x1_ref, g_ref, o_ref, acc_ref):
    j = pl.program_id(1)
    h = h_ref[...]
    gate = jnp.dot(h, wg_ref[...], preferred_element_type=F32)
    up = jnp.dot(h, wu_ref[...], preferred_element_type=F32)
    act = (gate * jax.nn.sigmoid(gate) * up).astype(BF16)
    part = jnp.dot(act, wd_ref[...], preferred_element_type=F32)

    @pl.when(j == 0)
    def _():
        acc_ref[...] = part

    @pl.when(j > 0)
    def _():
        acc_ref[...] += part

    @pl.when(j == pl.num_programs(1) - 1)
    def _():
        o_ref[...] = x1_ref[...] + _rms(acc_ref[...], RMS_EPS) * g_ref[...]


def _ffn(h2, w_gate, w_up, w_down, x1, post_ffn_g):
    seq, d_model = x1.shape
    d_ff = w_gate.shape[1]
    tm = 512
    tf = 512
    return pl.pallas_call(
        _ffn_kernel,
        grid=(seq // tm, d_ff // tf),
        in_specs=[
            pl.BlockSpec((tm, d_model), lambda i, j: (i, 0)),
            pl.BlockSpec((d_model, tf), lambda i, j: (0, j)),
            pl.BlockSpec((d_model, tf), lambda i, j: (0, j)),
            pl.BlockSpec((tf, d_model), lambda i, j: (j, 0)),
            pl.BlockSpec((tm, d_model), lambda i, j: (i, 0)),
            pl.BlockSpec((1, d_model), lambda i, j: (0, 0)),
        ],
        out_specs=pl.BlockSpec((tm, d_model), lambda i, j: (i, 0)),
        out_shape=jax.ShapeDtypeStruct((seq, d_model), F32),
        scratch_shapes=[pltpu.VMEM((tm, d_model), F32)],
        compiler_params=_params(("arbitrary", "arbitrary")),
        name="ffn",
    )(h2, w_gate, w_up, w_down, x1, post_ffn_g)


def kernel(x, positions, pre_mix_g, w_in, lambda_q1, lambda_k1, lambda_q2, lambda_k2, subln_g,
           gmlp_ln_g, gmlp_ln_b, w_s, b_s, w_out, post_mix_g, pre_ffn_g, w_gate, w_up, w_down,
           post_ffn_g):
    batch, seq, d_model = x.shape
    assert batch == 1 and pre_mix_g.shape[0] == 1, "single sequence, single layer"
    assert seq % 1024 == 0 and d_model == 2048
    x2 = x[0]
    cos, sin = _rope_tables(positions[0])
    bs_full = jnp.repeat(b_s[0].T, LANES, axis=1)
    q, k, v, gm = _inproj(x2, pre_mix_g, w_in[0].astype(BF16), cos, sin, gmlp_ln_g, gmlp_ln_b,
                          w_s[0].astype(BF16), bs_full)
    attn = _attention(lambda_q1, lambda_k1, lambda_q2, lambda_k2, q, k, v, subln_g)
    x1, h2 = _outproj(attn, gm, w_out[0].astype(BF16), x2, post_mix_g, pre_ffn_g)
    out = _ffn(h2, w_gate[0].astype(BF16), w_up[0].astype(BF16), w_down[0].astype(BF16), x1,
               post_ffn_g)
    return out[None]
```

```python
import functools
import math

import jax
import jax.numpy as jnp
from jax import lax
from jax.experimental import pallas as pl
from jax.experimental.pallas import tpu as pltpu

N_DIFF_HEADS = 8
DIFF_HEAD_DIM = 64
DIFF_V_DIM = 2 * DIFF_HEAD_DIM
N_GMLP_GROUPS = 8
CHUNK = 128
ROPE_THETA = 10000.0
RMS_EPS = 1e-6
LN_EPS = 1e-5
SUBLN_EPS = 1e-5
LAMBDA_INIT = 0.8 - 0.6 * math.exp(-0.3 * 0)

LANES = 128
VMEM_LIMIT_BYTES = 56 * 1024 * 1024

F32 = jnp.float32
BF16 = jnp.bfloat16


def _rms(x, eps):
    return x * lax.rsqrt(jnp.mean(x * x, axis=-1, keepdims=True) + eps)


def _gelu(x):
    return 0.5 * x * (1.0 + lax.erf(x * (1.0 / math.sqrt(2.0))))


def _params(semantics):
    return pltpu.CompilerParams(dimension_semantics=semantics, vmem_limit_bytes=VMEM_LIMIT_BYTES)


def _rope_table_kernel(pos_ref, inv_ref, cos_ref, sin_ref):
    ang = pos_ref[...].astype(F32) * inv_ref[...]
    lane = lax.broadcasted_iota(jnp.int32, ang.shape, 1)
    first_half = (lane % DIFF_HEAD_DIM) < (DIFF_HEAD_DIM // 2)
    s = jnp.sin(ang)
    cos_ref[...] = jnp.cos(ang)
    sin_ref[...] = jnp.where(first_half, -s, s)


def _rope_tables(positions):
    seq = positions.shape[0]
    tm = 1024
    inv_freq = ROPE_THETA ** (-jnp.arange(0, DIFF_HEAD_DIM, 2, dtype=F32) / DIFF_HEAD_DIM)
    inv = jnp.tile(inv_freq, LANES // (DIFF_HEAD_DIM // 2))[None, :]
    return pl.pallas_call(
        _rope_table_kernel,
        grid=(seq // tm,),
        in_specs=[pl.BlockSpec((tm, 1), lambda i: (i, 0)), pl.BlockSpec((1, LANES), lambda i: (0, 0))],
        out_specs=[pl.BlockSpec((tm, LANES), lambda i: (i, 0))] * 2,
        out_shape=[jax.ShapeDtypeStruct((seq, LANES), F32)] * 2,
        compiler_params=_params(("arbitrary",)),
        name="rope_tables",
    )(positions[:, None], inv)


def _rope(x, cos, sin_signed):
    lane = lax.broadcasted_iota(jnp.int32, x.shape, 1)
    first_half = (lane % DIFF_HEAD_DIM) < (DIFF_HEAD_DIM // 2)
    half = DIFF_HEAD_DIM // 2
    partner = jnp.where(first_half, pltpu.roll(x, LANES - half, 1), pltpu.roll(x, half, 1))
    return x * cos + partner * sin_signed


def _inproj_kernel(x_ref, g_ref, w_ref, cos_ref, sin_ref, lng_ref, lnb_ref, ws_ref, bs_ref,
                   q_ref, k_ref, v_ref, gm_ref, h_ref, gu_ref, *, q_scale):
    j = pl.program_id(1)
    tm, width = q_ref.shape

    @pl.when(j == 0)
    def _():
        h_ref[...] = (_rms(x_ref[...], RMS_EPS) * g_ref[...]).astype(BF16)

    def proj():
        return jnp.dot(h_ref[...], w_ref[...], preferred_element_type=F32)

    def rope_store(out_ref, scale):
        acc = proj()
        cos = cos_ref[...]
        sin = sin_ref[...]
        for c in range(width // LANES):
            sl = slice(c * LANES, (c + 1) * LANES)
            out_ref[:, sl] = (_rope(acc[:, sl], cos, sin) * scale).astype(BF16)

    @pl.when(j == 0)
    def _():
        rope_store(q_ref, q_scale)

    @pl.when(j == 1)
    def _():
        rope_store(k_ref, 1.0)

    @pl.when(j == 2)
    def _():
        v_ref[...] = proj().astype(BF16)

    @pl.when(j == 3)
    def _():
        gu_ref[...] = _gelu(proj())

    @pl.when(j == 4)
    def _():
        gv = _gelu(proj())
        mu = jnp.mean(gv, axis=-1, keepdims=True)
        d = gv - mu
        var = jnp.mean(d * d, axis=-1, keepdims=True)
        vln = (d * lax.rsqrt(var + LN_EPS) * lng_ref[...] + lnb_ref[...]).astype(BF16)
        for c in range(tm // CHUNK):
            rows = slice(c * CHUNK, (c + 1) * CHUNK)
            for g in range(N_GMLP_GROUPS):
                cols = slice(g * LANES, (g + 1) * LANES)
                y = jnp.dot(ws_ref[g], vln[rows, cols], preferred_element_type=F32) + bs_ref[:, cols]
                gm_ref[rows, cols] = (gu_ref[rows, cols] * y).astype(BF16)


def _inproj(x, pre_g, w_in, cos, sin, ln_g, ln_b, w_s, bs_full):
    seq, d_model = x.shape
    width = 1024
    tm = 512
    n_sections = w_in.shape[1] // width
    row_spec = lambda cols: pl.BlockSpec((tm, cols), lambda i, j: (i, 0))
    const_spec = lambda shape: pl.BlockSpec(shape, lambda i, j: (0,) * len(shape))
    return pl.pallas_call(
        functools.partial(_inproj_kernel, q_scale=DIFF_HEAD_DIM ** -0.5 * math.log2(math.e)),
        grid=(seq // tm, n_sections),
        in_specs=[
            row_spec(d_model),
            const_spec((1, d_model)),
            pl.BlockSpec((d_model, width), lambda i, j: (0, j)),
            row_spec(LANES),
            row_spec(LANES),
            const_spec((1, width)),
            const_spec((1, width)),
            const_spec(w_s.shape),
            const_spec(bs_full.shape),
        ],
        out_specs=[row_spec(width)] * 4,
        out_shape=[jax.ShapeDtypeStruct((seq, width), BF16)] * 4,
        scratch_shapes=[pltpu.VMEM((tm, d_model), BF16), pltpu.VMEM((tm, width), F32)],
        compiler_params=_params(("arbitrary", "arbitrary")),
        name="inproj",
    )(x, pre_g, w_in, cos, sin, ln_g, ln_b, w_s, bs_full)


def _attn_kernel(lq1_ref, lk1_ref, lq2_ref, lk2_ref, q_ref, k_ref, v_ref, g_ref, o_ref,
                 p_ref, shift_ref, lpart_ref, acc_ref, knorm_ref, *, tkc, row_block, unroll):
    tq = q_ref.shape[0]
    nkc = k_ref.shape[0] // tkc
    lane = lax.broadcasted_iota(jnp.int32, (tq, LANES), 1)
    in_map1 = lane < DIFF_HEAD_DIM
    q = q_ref[...]
    zero = jnp.zeros_like(q)
    q1 = jnp.where(in_map1, q, zero)
    q2 = jnp.where(in_map1, zero, q)
    nt_dims = (((1,), (1,)), ((), ()))

    def key_rows(kc):
        return pl.ds(pl.multiple_of(kc * tkc, tkc), tkc)

    row_blocks = [slice(r, r + row_block) for r in range(0, tq, row_block)]
    lane_tiles = [slice(t, t + LANES) for t in range(0, tkc, LANES)]

    def sweep(body, unroll):
        lax.fori_loop(0, nkc, lambda kc, c: (body(kc), c)[1], 0, unroll=unroll)

    def map_sq_norms(x):
        sq = x.astype(F32)
        sq = sq * sq
        return (jnp.sum(jnp.where(in_map1[:x.shape[0]], sq, 0.0), axis=1, keepdims=True),
                jnp.sum(jnp.where(in_map1[:x.shape[0]], 0.0, sq), axis=1, keepdims=True))

    @pl.when(pl.program_id(1) == 0)
    def _():
        def chunk_max(kc, carry):
            n1, n2 = map_sq_norms(k_ref[pl.ds(pl.multiple_of(kc * tq, tq), tq), :])
            return (jnp.maximum(carry[0], jnp.max(n1, axis=0, keepdims=True)),
                    jnp.maximum(carry[1], jnp.max(n2, axis=0, keepdims=True)))

        zero11 = jnp.zeros((1, 1), F32)
        k1, k2 = lax.fori_loop(0, k_ref.shape[0] // tq, chunk_max, (zero11, zero11))
        knorm_ref[0] = jnp.broadcast_to(jnp.sqrt(k1), (8, LANES))
        knorm_ref[1] = jnp.broadcast_to(jnp.sqrt(k2), (8, LANES))

    lam = (jnp.exp(jnp.sum(lq1_ref[...] * lk1_ref[...], axis=1, keepdims=True))
           - jnp.exp(jnp.sum(lq2_ref[...] * lk2_ref[...], axis=1, keepdims=True)) + LAMBDA_INIT)

    def emit(out):
        o_ref[...] = (_rms(out, SUBLN_EPS) * g_ref[...] * (1.0 - LAMBDA_INIT)).astype(BF16)

    def row_sums():
        return (jnp.sum(lpart_ref[0], axis=1, keepdims=True),
                jnp.sum(lpart_ref[1], axis=1, keepdims=True))

    q_maps = jnp.concatenate([q1, q2], axis=0)

    def map_scores(kc):
        return lax.dot_general(q_maps, k_ref[key_rows(kc), :], nt_dims, preferred_element_type=F32)

    def map_tiles(update):
        for mp in range(2):
            for rows in row_blocks:
                update(mp, rows, slice(mp * tq + rows.start, mp * tq + rows.stop))

    def shifted_probs(kc):
        s = map_scores(kc)

        def update(mp, rows, s_rows):
            shift = shift_ref[mp, rows]
            l = lpart_ref[mp, rows]
            for t in lane_tiles:
                p = jnp.exp2(s[s_rows, t] - shift)
                p_ref[mp, kc, rows, t] = p.astype(BF16)
                l = l + p
            lpart_ref[mp, rows] = l

        map_tiles(update)

    qn1, qn2 = map_sq_norms(q)
    shift_ref[0] = jnp.sqrt(qn1) * knorm_ref[0, 0:1, :]
    shift_ref[1] = jnp.sqrt(qn2) * knorm_ref[1, 0:1, :]
    lpart_ref[...] = jnp.zeros_like(lpart_ref)
    sweep(shifted_probs, unroll)
    bounded_sums = row_sums()

    def redo_with_row_max():
        lpart_ref[...] = jnp.full(lpart_ref.shape, -jnp.inf, F32)

        def row_max(kc):
            s = map_scores(kc)

            def update(mp, rows, s_rows):
                m = lpart_ref[mp, rows]
                for t in lane_tiles:
                    m = jnp.maximum(m, s[s_rows, t])
                lpart_ref[mp, rows] = m

            map_tiles(update)

        sweep(row_max, 1)
        for mp in range(2):
            shift_ref[mp] = jnp.broadcast_to(jnp.max(lpart_ref[mp], axis=1, keepdims=True), (tq, LANES))
        lpart_ref[...] = jnp.zeros_like(lpart_ref)
        sweep(shifted_probs, 1)
        return row_sums()

    sums_ok = jnp.minimum(jnp.min(bounded_sums[0]), jnp.min(bounded_sums[1])) >= 2.0 ** -60
    l1, l2 = lax.cond(sums_ok, lambda: bounded_sums, redo_with_row_max)

    ratio = jnp.broadcast_to(lam * l1 / l2, (tq, LANES)).astype(BF16)
    acc_ref[...] = jnp.zeros_like(acc_ref)

    def combined_values(kc):
        w = jnp.concatenate([p_ref[0, kc, :, t] - p_ref[1, kc, :, t] * ratio for t in lane_tiles], axis=1)
        acc_ref[...] += jnp.dot(w, v_ref[key_rows(kc), :], preferred_element_type=F32)

    sweep(combined_values, unroll)
    emit(acc_ref[...] * (1.0 / l1))


def _attention(lq1, lk1, lq2, lk2, q, k, v, subln_g):
    seq = q.shape[0]
    tq = 1024
    tkc = 512
    lam_spec = pl.BlockSpec((1, DIFF_HEAD_DIM), lambda h, i: (0, 0))
    return pl.pallas_call(
        functools.partial(_attn_kernel, tkc=tkc, row_block=64, unroll=4),
        grid=(N_DIFF_HEADS, seq // tq),
        in_specs=[
            lam_spec, lam_spec, lam_spec, lam_spec,
            pl.BlockSpec((tq, DIFF_V_DIM), lambda h, i: (i, h)),
            pl.BlockSpec((seq, DIFF_V_DIM), lambda h, i: (0, h)),
            pl.BlockSpec((seq, DIFF_V_DIM), lambda h, i: (0, h)),
            pl.BlockSpec((1, DIFF_V_DIM), lambda h, i: (0, 0)),
        ],
        out_specs=pl.BlockSpec((tq, DIFF_V_DIM), lambda h, i: (i, h)),
        out_shape=jax.ShapeDtypeStruct((seq, N_DIFF_HEADS * DIFF_V_DIM), BF16),
        scratch_shapes=[pltpu.VMEM((2, seq // tkc, tq, tkc), BF16),
                        pltpu.VMEM((2, tq, LANES), F32),
                        pltpu.VMEM((2, tq, LANES), F32),
                        pltpu.VMEM((tq, LANES), F32),
                        pltpu.VMEM((2, 8, LANES), F32)],
        compiler_params=_params(("arbitrary", "arbitrary")),
        name="diff_attention",
    )(lq1, lk1, lq2, lk2, q, k, v, subln_g)


def _outproj_kernel(a_ref, gm_ref, w_ref, x_ref, gpost_ref, gffn_ref, x1_ref, h2_ref):
    half = a_ref.shape[1]
    mix = (jnp.dot(a_ref[...], w_ref[0:half, :], preferred_element_type=F32)
           + jnp.dot(gm_ref[...], w_ref[half:2 * half, :], preferred_element_type=F32))
    x1 = x_ref[...] + _rms(mix, RMS_EPS) * gpost_ref[...]
    x1_ref[...] = x1
    h2_ref[...] = (_rms(x1, RMS_EPS) * gffn_ref[...]).astype(BF16)


def _outproj(attn, gm, w_out, x, post_mix_g, pre_ffn_g):
    seq, d_model = x.shape
    half = attn.shape[1]
    tm = 512
    row_spec = lambda cols: pl.BlockSpec((tm, cols), lambda i: (i, 0))
    const_spec = lambda shape: pl.BlockSpec(shape, lambda i: (0, 0))
    return pl.pallas_call(
        _outproj_kernel,
        grid=(seq // tm,),
        in_specs=[row_spec(half), row_spec(half), const_spec(w_out.shape), row_spec(d_model),
                  const_spec((1, d_model)), const_spec((1, d_model))],
        out_specs=[row_spec(d_model), row_spec(d_model)],
        out_shape=[jax.ShapeDtypeStruct((seq, d_model), F32), jax.ShapeDtypeStruct((seq, d_model), BF16)],
        compiler_params=_params(("arbitrary",)),
        name="outproj",
    )(attn, gm, w_out, x, post_mix_g, pre_ffn_g)


def _ffn_kernel(h_ref, wg_ref, wu_ref, wd_ref, x1_ref, g_ref, o_ref, acc_ref):
    j = pl.program_id(1)

    @pl.when(j == 0)
    def _():
        acc_ref[...] = jnp.zeros_like(acc_ref)

    h = h_ref[...]
    gate = jnp.dot(h, wg_ref[...], preferred_element_type=F32)
    up = jnp.dot(h, wu_ref[...], preferred_element_type=F32)
    act = (gate * jax.nn.sigmoid(gate) * up).astype(BF16)
    acc_ref[...] += jnp.dot(act, wd_ref[...], preferred_element_type=F32)

    @pl.when(j == pl.num_programs(1) - 1)
    def _():
        o_ref[...] = x1_ref[...] + _rms(acc_ref[...], RMS_EPS) * g_ref[...]


def _ffn(h2, w_gate, w_up, w_down, x1, post_ffn_g):
    seq, d_model = x1.shape
    d_ff = w_gate.shape[1]
    tm = 512
    tf = 512
    return pl.pallas_call(
        _ffn_kernel,
        grid=(seq // tm, d_ff // tf),
        in_specs=[
            pl.BlockSpec((tm, d_model), lambda i, j: (i, 0)),
            pl.BlockSpec((d_model, tf), lambda i, j: (0, j)),
            pl.BlockSpec((d_model, tf), lambda i, j: (0, j)),
            pl.BlockSpec((tf, d_model), lambda i, j: (j, 0)),
            pl.BlockSpec((tm, d_model), lambda i, j: (i, 0)),
            pl.BlockSpec((1, d_model), lambda i, j: (0, 0)),
        ],
        out_specs=pl.BlockSpec((tm, d_model), lambda i, j: (i, 0)),
        out_shape=jax.ShapeDtypeStruct((seq, d_model), F32),
        scratch_shapes=[pltpu.VMEM((tm, d_model), F32)],
        compiler_params=_params(("arbitrary", "arbitrary")),
        name="ffn",
    )(h2, w_gate, w_up, w_down, x1, post_ffn_g)


def kernel(x, positions, pre_mix_g, w_in, lambda_q1, lambda_k1, lambda_q2, lambda_k2, subln_g,
           gmlp_ln_g, gmlp_ln_b, w_s, b_s, w_out, post_mix_g, pre_ffn_g, w_gate, w_up, w_down,
           post_ffn_g):
    batch, seq, d_model = x.shape
    assert batch == 1 and pre_mix_g.shape[0] == 1, "single sequence, single layer"
    assert seq % 1024 == 0 and d_model == 2048
    x2 = x[0]
    cos, sin = _rope_tables(positions[0])
    bs_full = jnp.repeat(b_s[0].T, LANES, axis=1)
    q, k, v, gm = _inproj(x2, pre_mix_g, w_in[0].astype(BF16), cos, sin, gmlp_ln_g, gmlp_ln_b,
                          w_s[0].astype(BF16), bs_full)
    attn = _attention(lambda_q1, lambda_k1, lambda_q2, lambda_k2, q, k, v, subln_g)
    x1, h2 = _outproj(attn, gm, w_out[0].astype(BF16), x2, post_mix_g, pre_ffn_g)
    out = _ffn(h2, w_gate[0].astype(BF16), w_up[0].astype(BF16), w_down[0].astype(BF16), x1,
               post_ffn_g)
    return out[None]
```

```python
import functools
import math

import jax
import jax.numpy as jnp
from jax import lax
from jax.experimental import pallas as pl
from jax.experimental.pallas import tpu as pltpu

N_DIFF_HEADS = 8
DIFF_HEAD_DIM = 64
DIFF_V_DIM = 2 * DIFF_HEAD_DIM
N_GMLP_GROUPS = 8
CHUNK = 128
ROPE_THETA = 10000.0
RMS_EPS = 1e-6
LN_EPS = 1e-5
SUBLN_EPS = 1e-5
LAMBDA_INIT = 0.8 - 0.6 * math.exp(-0.3 * 0)

LANES = 128
VMEM_LIMIT_BYTES = 56 * 1024 * 1024

F32 = jnp.float32
BF16 = jnp.bfloat16


def _rms(x, eps):
    return x * lax.rsqrt(jnp.mean(x * x, axis=-1, keepdims=True) + eps)


def _gelu(x):
    return 0.5 * x * (1.0 + lax.erf(x * (1.0 / math.sqrt(2.0))))


def _params(semantics):
    return pltpu.CompilerParams(dimension_semantics=semantics, vmem_limit_bytes=VMEM_LIMIT_BYTES)


def _first_half_lanes(shape):
    lane = lax.broadcasted_iota(jnp.int32, shape, 1)
    return (lane % DIFF_HEAD_DIM) < (DIFF_HEAD_DIM // 2)


def _rope(x, cos, sin_signed):
    half = DIFF_HEAD_DIM // 2
    partner = jnp.where(_first_half_lanes(x.shape), pltpu.roll(x, LANES - half, 1), pltpu.roll(x, half, 1))
    return x * cos + partner * sin_signed


def _inproj_kernel(x_ref, g_ref, w_ref, pos_ref, inv_ref, lng_ref, lnb_ref, ws_ref, bs_ref,
                   q_ref, k_ref, v_ref, gm_ref, h_ref, gu_ref, cos_ref, sin_ref, *, q_scale):
    j = pl.program_id(1)
    tm, width = q_ref.shape

    def proj():
        return jnp.dot(h_ref[...], w_ref[...], preferred_element_type=F32)

    def rope_store(out_ref, scale):
        acc = proj()
        cos = cos_ref[...]
        sin = sin_ref[...]
        for c in range(width // LANES):
            sl = slice(c * LANES, (c + 1) * LANES)
            out_ref[:, sl] = (_rope(acc[:, sl], cos, sin) * scale).astype(BF16)

    @pl.when(j == 0)
    def _():
        h_ref[...] = (_rms(x_ref[...], RMS_EPS) * g_ref[...]).astype(BF16)
        v_ref[...] = proj().astype(BF16)
        ang = pos_ref[...].astype(F32) * inv_ref[...]
        s = jnp.sin(ang)
        cos_ref[...] = jnp.cos(ang)
        sin_ref[...] = jnp.where(_first_half_lanes(ang.shape), -s, s)

    @pl.when(j == 1)
    def _():
        rope_store(q_ref, q_scale)

    @pl.when(j == 2)
    def _():
        rope_store(k_ref, 1.0)

    @pl.when(j == 3)
    def _():
        gu_ref[...] = _gelu(proj())

    @pl.when(j == 4)
    def _():
        gv = _gelu(proj())
        mu = jnp.mean(gv, axis=-1, keepdims=True)
        d = gv - mu
        var = jnp.mean(d * d, axis=-1, keepdims=True)
        vln = (d * lax.rsqrt(var + LN_EPS) * lng_ref[...] + lnb_ref[...]).astype(BF16)
        for c in range(tm // CHUNK):
            rows = slice(c * CHUNK, (c + 1) * CHUNK)
            for g in range(N_GMLP_GROUPS):
                cols = slice(g * LANES, (g + 1) * LANES)
                y = jnp.dot(ws_ref[g], vln[rows, cols], preferred_element_type=F32) + bs_ref[:, cols]
                gm_ref[rows, cols] = (gu_ref[rows, cols] * y).astype(BF16)


def _inproj(x, pre_g, w_in, positions, inv_lanes, ln_g, ln_b, w_s, bs_full):
    seq, d_model = x.shape
    width = 1024
    tm = 512
    n_sections = w_in.shape[1] // width
    row_spec = lambda cols: pl.BlockSpec((tm, cols), lambda i, j: (i, 0))
    const_spec = lambda shape: pl.BlockSpec(shape, lambda i, j: (0,) * len(shape))
    return pl.pallas_call(
        functools.partial(_inproj_kernel, q_scale=DIFF_HEAD_DIM ** -0.5 * math.log2(math.e)),
        grid=(seq // tm, n_sections),
        in_specs=[
            row_spec(d_model),
            const_spec((1, d_model)),
            pl.BlockSpec((d_model, width), lambda i, j: (0, jnp.where(j < 3, (j + 2) % 3, j))),
            row_spec(1),
            const_spec((1, LANES)),
            const_spec((1, width)),
            const_spec((1, width)),
            const_spec(w_s.shape),
            const_spec(bs_full.shape),
        ],
        out_specs=[row_spec(width)] * 4,
        out_shape=[jax.ShapeDtypeStruct((seq, width), BF16)] * 4,
        scratch_shapes=[pltpu.VMEM((tm, d_model), BF16), pltpu.VMEM((tm, width), F32),
                        pltpu.VMEM((tm, LANES), F32), pltpu.VMEM((tm, LANES), F32)],
        compiler_params=_params(("arbitrary", "arbitrary")),
        name="inproj",
    )(x, pre_g, w_in, positions, inv_lanes, ln_g, ln_b, w_s, bs_full)


def _attn_kernel(*refs, tkc, row_block, unroll, cast_blocks):
    n_cast = len(cast_blocks)
    lq1_ref, lk1_ref, lq2_ref, lk2_ref, q_ref, k_ref, v_ref, g_ref = refs[:8]
    cast_src = refs[8:8 + n_cast]
    o_ref = refs[8 + n_cast]
    cast_dst = refs[9 + n_cast:9 + 2 * n_cast]
    p_ref, shift_ref, lpart_ref, acc_ref, knorm_ref = refs[9 + 2 * n_cast:]

    step = pl.program_id(0) * pl.num_programs(1) + pl.program_id(1)
    for src, dst, n_blocks in zip(cast_src, cast_dst, cast_blocks):
        @pl.when(step < n_blocks)
        def _(src=src, dst=dst):
            dst[...] = src[...].astype(BF16)

    tq = q_ref.shape[0]
    nkc = k_ref.shape[0] // tkc
    lane = lax.broadcasted_iota(jnp.int32, (tq, LANES), 1)
    in_map1 = lane < DIFF_HEAD_DIM
    q = q_ref[...]
    zero = jnp.zeros_like(q)
    q1 = jnp.where(in_map1, q, zero)
    q2 = jnp.where(in_map1, zero, q)
    nt_dims = (((1,), (1,)), ((), ()))

    def key_rows(kc):
        return pl.ds(pl.multiple_of(kc * tkc, tkc), tkc)

    row_blocks = [slice(r, r + row_block) for r in range(0, tq, row_block)]
    lane_tiles = [slice(t, t + LANES) for t in range(0, tkc, LANES)]

    def sweep(body, unroll):
        lax.fori_loop(0, nkc, lambda kc, c: (body(kc), c)[1], 0, unroll=unroll)

    def map_sq_norms(x):
        sq = x.astype(F32)
        sq = sq * sq
        return (jnp.sum(jnp.where(in_map1[:x.shape[0]], sq, 0.0), axis=1, keepdims=True),
                jnp.sum(jnp.where(in_map1[:x.shape[0]], 0.0, sq), axis=1, keepdims=True))

    @pl.when(pl.program_id(1) == 0)
    def _():
        def chunk_max(kc, carry):
            n1, n2 = map_sq_norms(k_ref[pl.ds(pl.multiple_of(kc * tq, tq), tq), :])
            return (jnp.maximum(carry[0], jnp.max(n1, axis=0, keepdims=True)),
                    jnp.maximum(carry[1], jnp.max(n2, axis=0, keepdims=True)))

        zero11 = jnp.zeros((1, 1), F32)
        k1, k2 = lax.fori_loop(0, k_ref.shape[0] // tq, chunk_max, (zero11, zero11))
        knorm_ref[0] = jnp.broadcast_to(jnp.sqrt(k1), (8, LANES))
        knorm_ref[1] = jnp.broadcast_to(jnp.sqrt(k2), (8, LANES))

    lam = (jnp.exp(jnp.sum(lq1_ref[...] * lk1_ref[...], axis=1, keepdims=True))
           - jnp.exp(jnp.sum(lq2_ref[...] * lk2_ref[...], axis=1, keepdims=True)) + LAMBDA_INIT)

    def emit(out):
        o_ref[...] = (_rms(out, SUBLN_EPS) * g_ref[...] * (1.0 - LAMBDA_INIT)).astype(BF16)

    def row_sums():
        return (jnp.sum(lpart_ref[0], axis=1, keepdims=True),
                jnp.sum(lpart_ref[1], axis=1, keepdims=True))

    q_maps = jnp.concatenate([q1, q2], axis=0)

    def map_scores(kc):
        return lax.dot_general(q_maps, k_ref[key_rows(kc), :], nt_dims, preferred_element_type=F32)

    def map_tiles(update):
        for mp in range(2):
            for rows in row_blocks:
                update(mp, rows, slice(mp * tq + rows.start, mp * tq + rows.stop))

    def shifted_probs(kc):
        s = map_scores(kc)

        def update(mp, rows, s_rows):
            shift = shift_ref[mp, rows]
            l = lpart_ref[mp, rows]
            for t in lane_tiles:
                p = jnp.exp2(s[s_rows, t] - shift)
                p_ref[mp, kc, rows, t] = p.astype(BF16)
                l = l + p
            lpart_ref[mp, rows] = l

        map_tiles(update)

    qn1, qn2 = map_sq_norms(q)
    shift_ref[0] = jnp.sqrt(qn1) * knorm_ref[0, 0:1, :]
    shift_ref[1] = jnp.sqrt(qn2) * knorm_ref[1, 0:1, :]
    lpart_ref[...] = jnp.zeros_like(lpart_ref)
    sweep(shifted_probs, unroll)
    bounded_sums = row_sums()

    def redo_with_row_max():
        lpart_ref[...] = jnp.full(lpart_ref.shape, -jnp.inf, F32)

        def row_max(kc):
            s = map_scores(kc)

            def update(mp, rows, s_rows):
                m = lpart_ref[mp, rows]
                for t in lane_tiles:
                    m = jnp.maximum(m, s[s_rows, t])
                lpart_ref[mp, rows] = m

            map_tiles(update)

        sweep(row_max, 1)
        for mp in range(2):
            shift_ref[mp] = jnp.broadcast_to(jnp.max(lpart_ref[mp], axis=1, keepdims=True), (tq, LANES))
        lpart_ref[...] = jnp.zeros_like(lpart_ref)
        sweep(shifted_probs, 1)
        return row_sums()

    sums_ok = jnp.minimum(jnp.min(bounded_sums[0]), jnp.min(bounded_sums[1])) >= 2.0 ** -60
    l1, l2 = lax.cond(sums_ok, lambda: bounded_sums, redo_with_row_max)

    ratio = jnp.broadcast_to(lam * l1 / l2, (tq, LANES)).astype(BF16)
    acc_ref[...] = jnp.zeros_like(acc_ref)

    def combined_values(kc):
        w = jnp.concatenate([p_ref[0, kc, :, t] - p_ref[1, kc, :, t] * ratio for t in lane_tiles], axis=1)
        acc_ref[...] += jnp.dot(w, v_ref[key_rows(kc), :], preferred_element_type=F32)

    sweep(combined_values, unroll)
    emit(acc_ref[...] * (1.0 / l1))


BF16_SUBLANES = 16


def _cast_block_rows(rows, n_steps):
    for block in range(BF16_SUBLANES, rows + 1, BF16_SUBLANES):
        if rows % block == 0 and rows // block <= n_steps:
            return block
    raise ValueError(f"no row block for {rows} rows in {n_steps} steps")


def _attention(lq1, lk1, lq2, lk2, q, k, v, subln_g, f32_weights):
    seq = q.shape[0]
    tq = 1024
    tkc = 512
    n_q = seq // tq
    n_steps = N_DIFF_HEADS * n_q
    lam_spec = pl.BlockSpec((1, DIFF_HEAD_DIM), lambda h, i: (0, 0))
    cast_specs, cast_blocks = [], []
    for w in f32_weights:
        block = _cast_block_rows(w.shape[0], n_steps)
        n_blocks = w.shape[0] // block
        cast_blocks.append(n_blocks)
        cast_specs.append(pl.BlockSpec(
            (block, w.shape[1]), lambda h, i, last=n_blocks - 1: (jnp.minimum(h * n_q + i, last), 0)))
    outs = pl.pallas_call(
        functools.partial(_attn_kernel, tkc=tkc, row_block=64, unroll=4, cast_blocks=tuple(cast_blocks)),
        grid=(N_DIFF_HEADS, n_q),
        in_specs=[
            lam_spec, lam_spec, lam_spec, lam_spec,
            pl.BlockSpec((tq, DIFF_V_DIM), lambda h, i: (i, h)),
            pl.BlockSpec((seq, DIFF_V_DIM), lambda h, i: (0, h)),
            pl.BlockSpec((seq, DIFF_V_DIM), lambda h, i: (0, h)),
            pl.BlockSpec((1, DIFF_V_DIM), lambda h, i: (0, 0)),
        ] + cast_specs,
        out_specs=[pl.BlockSpec((tq, DIFF_V_DIM), lambda h, i: (i, h))] + cast_specs,
        out_shape=[jax.ShapeDtypeStruct((seq, N_DIFF_HEADS * DIFF_V_DIM), BF16)]
        + [jax.ShapeDtypeStruct(w.shape, BF16) for w in f32_weights],
        scratch_shapes=[pltpu.VMEM((2, seq // tkc, tq, tkc), BF16),
                        pltpu.VMEM((2, tq, LANES), F32),
                        pltpu.VMEM((2, tq, LANES), F32),
                        pltpu.VMEM((tq, LANES), F32),
                        pltpu.VMEM((2, 8, LANES), F32)],
        compiler_params=_params(("arbitrary", "arbitrary")),
        name="diff_attention",
    )(lq1, lk1, lq2, lk2, q, k, v, subln_g, *f32_weights)
    return outs[0], outs[1:]


def _outproj_kernel(a_ref, gm_ref, w_ref, x_ref, gpost_ref, gffn_ref, x1_ref, h2_ref):
    half = a_ref.shape[1]
    mix = (jnp.dot(a_ref[...], w_ref[0:half, :], preferred_element_type=F32)
           + jnp.dot(gm_ref[...], w_ref[half:2 * half, :], preferred_element_type=F32))
    x1 = x_ref[...] + _rms(mix, RMS_EPS) * gpost_ref[...]
    x1_ref[...] = x1
    h2_ref[...] = (_rms(x1, RMS_EPS) * gffn_ref[...]).astype(BF16)


def _outproj(attn, gm, w_out, x, post_mix_g, pre_ffn_g):
    seq, d_model = x.shape
    half = attn.shape[1]
    tm = 512
    row_spec = lambda cols: pl.BlockSpec((tm, cols), lambda i: (i, 0))
    const_spec = lambda shape: pl.BlockSpec(shape, lambda i: (0, 0))
    return pl.pallas_call(
        _outproj_kernel,
        grid=(seq // tm,),
        in_specs=[row_spec(half), row_spec(half), const_spec(w_out.shape), row_spec(d_model),
                  const_spec((1, d_model)), const_spec((1, d_model))],
        out_specs=[row_spec(d_model), row_spec(d_model)],
        out_shape=[jax.ShapeDtypeStruct((seq, d_model), F32), jax.ShapeDtypeStruct((seq, d_model), BF16)],
        compiler_params=_params(("arbitrary",)),
        name="outproj",
    )(attn, gm, w_out, x, post_mix_g, pre_ffn_g)


def _ffn_kernel(h_ref, wg_ref, wu_ref, wd_ref, x1_ref, g_ref, o_ref, acc_ref):
    j = pl.program_id(1)

    @pl.when(j == 0)
    def _():
        acc_ref[...] = jnp.zeros_like(acc_ref)

    h = h_ref[...]
    gate = jnp.dot(h, wg_ref[...], preferred_element_type=F32)
    up = jnp.dot(h, wu_ref[...], preferred_element_type=F32)
    act = (gate * jax.nn.sigmoid(gate) * up).astype(BF16)
    acc_ref[...] += jnp.dot(act, wd_ref[...], preferred_element_type=F32)

    @pl.when(j == pl.num_programs(1) - 1)
    def _():
        o_ref[...] = x1_ref[...] + _rms(acc_ref[...], RMS_EPS) * g_ref[...]


def _ffn(h2, w_gate, w_up, w_down, x1, post_ffn_g):
    seq, d_model = x1.shape
    d_ff = w_gate.shape[1]
    tm = 512
    tf = 512
    return pl.pallas_call(
        _ffn_kernel,
        grid=(seq // tm, d_ff // tf),
        in_specs=[
            pl.BlockSpec((tm, d_model), lambda i, j: (i, 0)),
            pl.BlockSpec((d_model, tf), lambda i, j: (0, j)),
            pl.BlockSpec((d_model, tf), lambda i, j: (0, j)),
            pl.BlockSpec((tf, d_model), lambda i, j: (j, 0)),
            pl.BlockSpec((tm, d_model), lambda i, j: (i, 0)),
            pl.BlockSpec((1, d_model), lambda i, j: (0, 0)),
        ],
        out_specs=pl.BlockSpec((tm, d_model), lambda i, j: (i, 0)),
        out_shape=jax.ShapeDtypeStruct((seq, d_model), F32),
        scratch_shapes=[pltpu.VMEM((tm, d_model), F32)],
        compiler_params=_params(("arbitrary", "arbitrary")),
        name="ffn",
    )(h2, w_gate, w_up, w_down, x1, post_ffn_g)


def kernel(x, positions, pre_mix_g, w_in, lambda_q1, lambda_k1, lambda_q2, lambda_k2, subln_g,
           gmlp_ln_g, gmlp_ln_b, w_s, b_s, w_out, post_mix_g, pre_ffn_g, w_gate, w_up, w_down,
           post_ffn_g):
    batch, seq, d_model = x.shape
    assert batch == 1 and pre_mix_g.shape[0] == 1, "single sequence, single layer"
    assert seq % 1024 == 0 and d_model == 2048
    x2 = x[0]
    inv_freq = ROPE_THETA ** (-jnp.arange(0, DIFF_HEAD_DIM, 2, dtype=F32) / DIFF_HEAD_DIM)
    inv_lanes = jnp.tile(inv_freq, LANES // (DIFF_HEAD_DIM // 2))[None, :]
    bs_full = jnp.repeat(b_s[0].T, LANES, axis=1)
    q, k, v, gm = _inproj(x2, pre_mix_g, w_in[0].astype(BF16), positions[0][:, None], inv_lanes,
                          gmlp_ln_g, gmlp_ln_b, w_s[0].astype(BF16), bs_full)
    attn, (w_out16, w_gate16, w_up16, w_down16) = _attention(
        lambda_q1, lambda_k1, lambda_q2, lambda_k2, q, k, v, subln_g,
        (w_out[0], w_gate[0], w_up[0], w_down[0]))
    x1, h2 = _outproj(attn, gm, w_out16, x2, post_mix_g, pre_ffn_g)
    out = _ffn(h2, w_gate16, w_up16, w_down16, x1, post_ffn_g)
    return out[None]
```

```python
import functools
import math

import jax
import jax.numpy as jnp
from jax import lax
from jax.experimental import pallas as pl
from jax.experimental.pallas import tpu as pltpu

N_DIFF_HEADS = 8
DIFF_HEAD_DIM = 64
DIFF_V_DIM = 2 * DIFF_HEAD_DIM
N_GMLP_GROUPS = 8
CHUNK = 128
ROPE_THETA = 10000.0
RMS_EPS = 1e-6
LN_EPS = 1e-5
SUBLN_EPS = 1e-5
LAMBDA_INIT = 0.8 - 0.6 * math.exp(-0.3 * 0)

LANES = 128
VMEM_LIMIT_BYTES = 56 * 1024 * 1024

F32 = jnp.float32
BF16 = jnp.bfloat16


def _rms(x, eps):
    return x * lax.rsqrt(jnp.mean(x * x, axis=-1, keepdims=True) + eps)


def _gelu(x):
    return 0.5 * x * (1.0 + lax.erf(x * (1.0 / math.sqrt(2.0))))


def _params(semantics):
    return pltpu.CompilerParams(dimension_semantics=semantics, vmem_limit_bytes=VMEM_LIMIT_BYTES)


def _first_half_lanes(shape):
    lane = lax.broadcasted_iota(jnp.int32, shape, 1)
    return (lane % DIFF_HEAD_DIM) < (DIFF_HEAD_DIM // 2)


def _rope(x, cos, sin_signed):
    half = DIFF_HEAD_DIM // 2
    partner = jnp.where(_first_half_lanes(x.shape), pltpu.roll(x, LANES - half, 1), pltpu.roll(x, half, 1))
    return x * cos + partner * sin_signed


def _inproj_kernel(x_ref, g_ref, w_ref, pos_ref, inv_ref, lng_ref, lnb_ref, ws_ref, bs_ref,
                   q_ref, k_ref, v_ref, gm_ref, h_ref, gu_ref, *, q_scale):
    tm, width = q_ref.shape
    q_cols, k_cols, v_cols, gu_cols, gv_cols = (slice(n * width, (n + 1) * width) for n in range(5))

    def proj(cols):
        return jnp.dot(h_ref[...], w_ref[:, cols], preferred_element_type=F32)

    h_ref[...] = (_rms(x_ref[...], RMS_EPS) * g_ref[...]).astype(BF16)
    v_ref[...] = proj(v_cols).astype(BF16)

    ang = pos_ref[...].astype(F32) * inv_ref[...]
    sin = jnp.sin(ang)
    sin = jnp.where(_first_half_lanes(ang.shape), -sin, sin)
    cos = jnp.cos(ang)

    def rope_store(out_ref, cols, scale):
        acc = proj(cols)
        for c in range(width // LANES):
            sl = slice(c * LANES, (c + 1) * LANES)
            out_ref[:, sl] = (_rope(acc[:, sl], cos, sin) * scale).astype(BF16)

    rope_store(q_ref, q_cols, q_scale)
    rope_store(k_ref, k_cols, 1.0)

    gu_ref[...] = _gelu(proj(gu_cols))
    gv = _gelu(proj(gv_cols))
    mu = jnp.mean(gv, axis=-1, keepdims=True)
    d = gv - mu
    var = jnp.mean(d * d, axis=-1, keepdims=True)
    vln = (d * lax.rsqrt(var + LN_EPS) * lng_ref[...] + lnb_ref[...]).astype(BF16)
    for c in range(tm // CHUNK):
        rows = slice(c * CHUNK, (c + 1) * CHUNK)
        for g in range(N_GMLP_GROUPS):
            cols = slice(g * LANES, (g + 1) * LANES)
            y = jnp.dot(ws_ref[g], vln[rows, cols], preferred_element_type=F32) + bs_ref[:, cols]
            gm_ref[rows, cols] = (gu_ref[rows, cols] * y).astype(BF16)


def _inproj(x, pre_g, w_in, positions, inv_lanes, ln_g, ln_b, w_s, bs_full):
    seq, d_model = x.shape
    width = w_in.shape[1] // 5
    tm = 512
    row_spec = lambda cols: pl.BlockSpec((tm, cols), lambda i: (i, 0))
    const_spec = lambda shape, **kw: pl.BlockSpec(shape, lambda i: (0,) * len(shape), **kw)
    return pl.pallas_call(
        functools.partial(_inproj_kernel, q_scale=DIFF_HEAD_DIM ** -0.5 * math.log2(math.e)),
        grid=(seq // tm,),
        in_specs=[
            row_spec(d_model),
            const_spec((1, d_model)),
            const_spec(w_in.shape, pipeline_mode=pl.Buffered(1)),
            row_spec(1),
            const_spec((1, LANES)),
            const_spec((1, width)),
            const_spec((1, width)),
            const_spec(w_s.shape),
            const_spec(bs_full.shape),
        ],
        out_specs=[row_spec(width)] * 4,
        out_shape=[jax.ShapeDtypeStruct((seq, width), BF16)] * 4,
        scratch_shapes=[pltpu.VMEM((tm, d_model), BF16), pltpu.VMEM((tm, width), F32)],
        compiler_params=_params(("arbitrary",)),
        name="inproj",
    )(x, pre_g, w_in, positions, inv_lanes, ln_g, ln_b, w_s, bs_full)


def _attn_kernel(*refs, tkc, row_block, unroll, cast_blocks):
    n_cast = len(cast_blocks)
    lq1_ref, lk1_ref, lq2_ref, lk2_ref, q_ref, k_ref, v_ref, g_ref = refs[:8]
    cast_src = refs[8:8 + n_cast]
    o_ref = refs[8 + n_cast]
    cast_dst = refs[9 + n_cast:9 + 2 * n_cast]
    p_ref, shift_ref, lpart_ref, acc_ref, knorm_ref = refs[9 + 2 * n_cast:]

    step = pl.program_id(0) * pl.num_programs(1) + pl.program_id(1)
    for src, dst, n_blocks in zip(cast_src, cast_dst, cast_blocks):
        @pl.when(step < n_blocks)
        def _(src=src, dst=dst):
            dst[...] = src[...].astype(BF16)

    tq = q_ref.shape[0]
    nkc = k_ref.shape[0] // tkc
    lane = lax.broadcasted_iota(jnp.int32, (tq, LANES), 1)
    in_map1 = lane < DIFF_HEAD_DIM
    q = q_ref[...]
    zero = jnp.zeros_like(q)
    q1 = jnp.where(in_map1, q, zero)
    q2 = jnp.where(in_map1, zero, q)
    nt_dims = (((1,), (1,)), ((), ()))

    def key_rows(kc):
        return pl.ds(pl.multiple_of(kc * tkc, tkc), tkc)

    row_blocks = [slice(r, r + row_block) for r in range(0, tq, row_block)]
    lane_tiles = [slice(t, t + LANES) for t in range(0, tkc, LANES)]

    def sweep(body, unroll):
        lax.fori_loop(0, nkc, lambda kc, c: (body(kc), c)[1], 0, unroll=unroll)

    def map_sq_norms(x):
        sq = x.astype(F32)
        sq = sq * sq
        return (jnp.sum(jnp.where(in_map1[:x.shape[0]], sq, 0.0), axis=1, keepdims=True),
                jnp.sum(jnp.where(in_map1[:x.shape[0]], 0.0, sq), axis=1, keepdims=True))

    @pl.when(pl.program_id(1) == 0)
    def _():
        def chunk_max(kc, carry):
            n1, n2 = map_sq_norms(k_ref[pl.ds(pl.multiple_of(kc * tq, tq), tq), :])
            return (jnp.maximum(carry[0], jnp.max(n1, axis=0, keepdims=True)),
                    jnp.maximum(carry[1], jnp.max(n2, axis=0, keepdims=True)))

        zero11 = jnp.zeros((1, 1), F32)
        k1, k2 = lax.fori_loop(0, k_ref.shape[0] // tq, chunk_max, (zero11, zero11))
        knorm_ref[0] = jnp.broadcast_to(jnp.sqrt(k1), (8, LANES))
        knorm_ref[1] = jnp.broadcast_to(jnp.sqrt(k2), (8, LANES))

    lam = (jnp.exp(jnp.sum(lq1_ref[...] * lk1_ref[...], axis=1, keepdims=True))
           - jnp.exp(jnp.sum(lq2_ref[...] * lk2_ref[...], axis=1, keepdims=True)) + LAMBDA_INIT)

    def emit(out):
        o_ref[...] = (_rms(out, SUBLN_EPS) * g_ref[...] * (1.0 - LAMBDA_INIT)).astype(BF16)

    def row_sums():
        return (jnp.sum(lpart_ref[0], axis=1, keepdims=True),
                jnp.sum(lpart_ref[1], axis=1, keepdims=True))

    q_maps = jnp.concatenate([q1, q2], axis=0)

    def map_scores(kc):
        return lax.dot_general(q_maps, k_ref[key_rows(kc), :], nt_dims, preferred_element_type=F32)

    def map_tiles(update):
        for mp in range(2):
            for rows in row_blocks:
                update(mp, rows, slice(mp * tq + rows.start, mp * tq + rows.stop))

    def shifted_probs(kc):
        s = map_scores(kc)

        def update(mp, rows, s_rows):
            shift = shift_ref[mp, rows]
            l = lpart_ref[mp, rows]
            for t in lane_tiles:
                p = jnp.exp2(s[s_rows, t] - shift)
                p_ref[mp, kc, rows, t] = p.astype(BF16)
                l = l + p
            lpart_ref[mp, rows] = l

        map_tiles(update)

    qn1, qn2 = map_sq_norms(q)
    shift_ref[0] = jnp.sqrt(qn1) * knorm_ref[0, 0:1, :]
    shift_ref[1] = jnp.sqrt(qn2) * knorm_ref[1, 0:1, :]
    lpart_ref[...] = jnp.zeros_like(lpart_ref)
    sweep(shifted_probs, unroll)
    bounded_sums = row_sums()

    def redo_with_row_max():
        lpart_ref[...] = jnp.full(lpart_ref.shape, -jnp.inf, F32)

        def row_max(kc):
            s = map_scores(kc)

            def update(mp, rows, s_rows):
                m = lpart_ref[mp, rows]
                for t in lane_tiles:
                    m = jnp.maximum(m, s[s_rows, t])
                lpart_ref[mp, rows] = m

            map_tiles(update)

        sweep(row_max, 1)
        for mp in range(2):
            shift_ref[mp] = jnp.broadcast_to(jnp.max(lpart_ref[mp], axis=1, keepdims=True), (tq, LANES))
        lpart_ref[...] = jnp.zeros_like(lpart_ref)
        sweep(shifted_probs, 1)
        return row_sums()

    sums_ok = jnp.minimum(jnp.min(bounded_sums[0]), jnp.min(bounded_sums[1])) >= 2.0 ** -60
    l1, l2 = lax.cond(sums_ok, lambda: bounded_sums, redo_with_row_max)

    ratio = jnp.broadcast_to(lam * l1 / l2, (tq, LANES)).astype(BF16)
    acc_ref[...] = jnp.zeros_like(acc_ref)

    def combined_values(kc):
        w = jnp.concatenate([p_ref[0, kc, :, t] - p_ref[1, kc, :, t] * ratio for t in lane_tiles], axis=1)
        acc_ref[...] += jnp.dot(w, v_ref[key_rows(kc), :], preferred_element_type=F32)

    sweep(combined_values, unroll)
    emit(acc_ref[...] * (1.0 / l1))


BF16_SUBLANES = 16


def _cast_block_rows(rows, n_steps):
    for block in range(BF16_SUBLANES, rows + 1, BF16_SUBLANES):
        if rows % block == 0 and rows // block <= n_steps:
            return block
    raise ValueError(f"no row block for {rows} rows in {n_steps} steps")


def _attention(lq1, lk1, lq2, lk2, q, k, v, subln_g, f32_weights):
    seq = q.shape[0]
    tq = 1024
    tkc = 512
    n_q = seq // tq
    n_steps = N_DIFF_HEADS * n_q
    lam_spec = pl.BlockSpec((1, DIFF_HEAD_DIM), lambda h, i: (0, 0))
    cast_specs, cast_blocks = [], []
    for w in f32_weights:
        block = _cast_block_rows(w.shape[0], n_steps)
        n_blocks = w.shape[0] // block
        cast_blocks.append(n_blocks)
        cast_specs.append(pl.BlockSpec(
            (block, w.shape[1]), lambda h, i, last=n_blocks - 1: (jnp.minimum(h * n_q + i, last), 0)))
    outs = pl.pallas_call(
        functools.partial(_attn_kernel, tkc=tkc, row_block=64, unroll=4, cast_blocks=tuple(cast_blocks)),
        grid=(N_DIFF_HEADS, n_q),
        in_specs=[
            lam_spec, lam_spec, lam_spec, lam_spec,
            pl.BlockSpec((tq, DIFF_V_DIM), lambda h, i: (i, h)),
            pl.BlockSpec((seq, DIFF_V_DIM), lambda h, i: (0, h)),
            pl.BlockSpec((seq, DIFF_V_DIM), lambda h, i: (0, h)),
            pl.BlockSpec((1, DIFF_V_DIM), lambda h, i: (0, 0)),
        ] + cast_specs,
        out_specs=[pl.BlockSpec((tq, DIFF_V_DIM), lambda h, i: (i, h))] + cast_specs,
        out_shape=[jax.ShapeDtypeStruct((seq, N_DIFF_HEADS * DIFF_V_DIM), BF16)]
        + [jax.ShapeDtypeStruct(w.shape, BF16) for w in f32_weights],
        scratch_shapes=[pltpu.VMEM((2, seq // tkc, tq, tkc), BF16),
                        pltpu.VMEM((2, tq, LANES), F32),
                        pltpu.VMEM((2, tq, LANES), F32),
                        pltpu.VMEM((tq, LANES), F32),
                        pltpu.VMEM((2, 8, LANES), F32)],
        compiler_params=_params(("arbitrary", "arbitrary")),
        name="diff_attention",
    )(lq1, lk1, lq2, lk2, q, k, v, subln_g, *f32_weights)
    return outs[0], outs[1:]


def _outproj_kernel(a_ref, gm_ref, w_ref, x_ref, gpost_ref, gffn_ref, x1_ref, h2_ref):
    half = a_ref.shape[1]
    mix = (jnp.dot(a_ref[...], w_ref[0:half, :], preferred_element_type=F32)
           + jnp.dot(gm_ref[...], w_ref[half:2 * half, :], preferred_element_type=F32))
    x1 = x_ref[...] + _rms(mix, RMS_EPS) * gpost_ref[...]
    x1_ref[...] = x1
    h2_ref[...] = (_rms(x1, RMS_EPS) * gffn_ref[...]).astype(BF16)


def _outproj(attn, gm, w_out, x, post_mix_g, pre_ffn_g):
    seq, d_model = x.shape
    half = attn.shape[1]
    tm = 512
    row_spec = lambda cols: pl.BlockSpec((tm, cols), lambda i: (i, 0))
    const_spec = lambda shape: pl.BlockSpec(shape, lambda i: (0, 0))
    return pl.pallas_call(
        _outproj_kernel,
        grid=(seq // tm,),
        in_specs=[row_spec(half), row_spec(half), const_spec(w_out.shape), row_spec(d_model),
                  const_spec((1, d_model)), const_spec((1, d_model))],
        out_specs=[row_spec(d_model), row_spec(d_model)],
        out_shape=[jax.ShapeDtypeStruct((seq, d_model), F32), jax.ShapeDtypeStruct((seq, d_model), BF16)],
        compiler_params=_params(("arbitrary",)),
        name="outproj",
    )(attn, gm, w_out, x, post_mix_g, pre_ffn_g)


def _ffn_kernel(h_ref, wg_ref, wu_ref, wd_ref, x1_ref, g_ref, o_ref, acc_ref):
    j = pl.program_id(1)

    @pl.when(j == 0)
    def _():
        acc_ref[...] = jnp.zeros_like(acc_ref)

    h = h_ref[...]
    gate = jnp.dot(h, wg_ref[...], preferred_element_type=F32)
    up = jnp.dot(h, wu_ref[...], preferred_element_type=F32)
    act = (gate * jax.nn.sigmoid(gate) * up).astype(BF16)
    acc_ref[...] += jnp.dot(act, wd_ref[...], preferred_element_type=F32)

    @pl.when(j == pl.num_programs(1) - 1)
    def _():
        o_ref[...] = x1_ref[...] + _rms(acc_ref[...], RMS_EPS) * g_ref[...]


def _ffn(h2, w_gate, w_up, w_down, x1, post_ffn_g):
    seq, d_model = x1.shape
    d_ff = w_gate.shape[1]
    tm = 512
    tf = 512
    return pl.pallas_call(
        _ffn_kernel,
        grid=(seq // tm, d_ff // tf),
        in_specs=[
            pl.BlockSpec((tm, d_model), lambda i, j: (i, 0)),
            pl.BlockSpec((d_model, tf), lambda i, j: (0, j)),
            pl.BlockSpec((d_model, tf), lambda i, j: (0, j)),
            pl.BlockSpec((tf, d_model), lambda i, j: (j, 0)),
            pl.BlockSpec((tm, d_model), lambda i, j: (i, 0)),
            pl.BlockSpec((1, d_model), lambda i, j: (0, 0)),
        ],
        out_specs=pl.BlockSpec((tm, d_model), lambda i, j: (i, 0)),
        out_shape=jax.ShapeDtypeStruct((seq, d_model), F32),
        scratch_shapes=[pltpu.VMEM((tm, d_model), F32)],
        compiler_params=_params(("arbitrary", "arbitrary")),
        name="ffn",
    )(h2, w_gate, w_up, w_down, x1, post_ffn_g)


def kernel(x, positions, pre_mix_g, w_in, lambda_q1, lambda_k1, lambda_q2, lambda_k2, subln_g,
           gmlp_ln_g, gmlp_ln_b, w_s, b_s, w_out, post_mix_g, pre_ffn_g, w_gate, w_up, w_down,
           post_ffn_g):
    batch, seq, d_model = x.shape
    assert batch == 1 and pre_mix_g.shape[0] == 1, "single sequence, single layer"
    assert seq % 1024 == 0 and d_model == 2048
    x2 = x[0]
    inv_freq = ROPE_THETA ** (-jnp.arange(0, DIFF_HEAD_DIM, 2, dtype=F32) / DIFF_HEAD_DIM)
    inv_lanes = jnp.tile(inv_freq, LANES // (DIFF_HEAD_DIM // 2))[None, :]
    bs_full = jnp.repeat(b_s[0].T, LANES, axis=1)
    q, k, v, gm = _inproj(x2, pre_mix_g, w_in[0].astype(BF16), positions[0][:, None], inv_lanes,
                          gmlp_ln_g, gmlp_ln_b, w_s[0].astype(BF16), bs_full)
    attn, (w_out16, w_gate16, w_up16, w_down16) = _attention(
        lambda_q1, lambda_k1, lambda_q2, lambda_k2, q, k, v, subln_g,
        (w_out[0], w_gate[0], w_up[0], w_down[0]))
    x1, h2 = _outproj(attn, gm, w_out16, x2, post_mix_g, pre_ffn_g)
    out = _ffn(h2, w_gate16, w_up16, w_down16, x1, post_ffn_g)
    return out[None]
```

```python
import functools
import math

import jax
import jax.numpy as jnp
from jax import lax
from jax.experimental import pallas as pl
from jax.experimental.pallas import tpu as pltpu

N_DIFF_HEADS = 8
DIFF_HEAD_DIM = 64
DIFF_V_DIM = 2 * DIFF_HEAD_DIM
N_GMLP_GROUPS = 8
CHUNK = 128
ROPE_THETA = 10000.0
RMS_EPS = 1e-6
LN_EPS = 1e-5
SUBLN_EPS = 1e-5
LAMBDA_INIT = 0.8 - 0.6 * math.exp(-0.3 * 0)

LANES = 128
VMEM_LIMIT_BYTES = 56 * 1024 * 1024

F32 = jnp.float32
BF16 = jnp.bfloat16


def _rms(x, eps):
    return x * lax.rsqrt(jnp.mean(x * x, axis=-1, keepdims=True) + eps)


def _gelu(x):
    return 0.5 * x * (1.0 + lax.erf(x * (1.0 / math.sqrt(2.0))))


def _params(semantics):
    return pltpu.CompilerParams(dimension_semantics=semantics, vmem_limit_bytes=VMEM_LIMIT_BYTES)


def _first_half_lanes(shape):
    lane = lax.broadcasted_iota(jnp.int32, shape, 1)
    return (lane % DIFF_HEAD_DIM) < (DIFF_HEAD_DIM // 2)


def _rope(x, cos, sin_signed):
    half = DIFF_HEAD_DIM // 2
    partner = jnp.where(_first_half_lanes(x.shape), pltpu.roll(x, LANES - half, 1), pltpu.roll(x, half, 1))
    return x * cos + partner * sin_signed


def _inproj_kernel(x_ref, g_ref, w_ref, pos_ref, inv_ref, lng_ref, lnb_ref, ws_ref, bs_ref,
                   q_ref, k_ref, v_ref, gm_ref, h_ref, gu_ref, *, q_scale):
    tm, width = q_ref.shape
    q_cols, k_cols, v_cols, gu_cols, gv_cols = (slice(n * width, (n + 1) * width) for n in range(5))

    def proj(cols):
        return jnp.dot(h_ref[...], w_ref[:, cols], preferred_element_type=F32)

    h_ref[...] = (_rms(x_ref[...], RMS_EPS) * g_ref[...]).astype(BF16)
    v_ref[...] = proj(v_cols).astype(BF16)

    ang = pos_ref[...].astype(F32) * inv_ref[...]
    sin = jnp.sin(ang)
    sin = jnp.where(_first_half_lanes(ang.shape), -sin, sin)
    cos = jnp.cos(ang)

    def rope_store(out_ref, cols, scale):
        acc = proj(cols)
        for c in range(width // LANES):
            sl = slice(c * LANES, (c + 1) * LANES)
            out_ref[:, sl] = (_rope(acc[:, sl], cos, sin) * scale).astype(BF16)

    rope_store(q_ref, q_cols, q_scale)
    rope_store(k_ref, k_cols, 1.0)

    gu_ref[...] = _gelu(proj(gu_cols))
    gv = _gelu(proj(gv_cols))
    mu = jnp.mean(gv, axis=-1, keepdims=True)
    d = gv - mu
    var = jnp.mean(d * d, axis=-1, keepdims=True)
    vln = (d * lax.rsqrt(var + LN_EPS) * lng_ref[...] + lnb_ref[...]).astype(BF16)
    for c in range(tm // CHUNK):
        rows = slice(c * CHUNK, (c + 1) * CHUNK)
        for g in range(N_GMLP_GROUPS):
            cols = slice(g * LANES, (g + 1) * LANES)
            y = jnp.dot(ws_ref[g], vln[rows, cols], preferred_element_type=F32) + bs_ref[:, cols]
            gm_ref[rows, cols] = (gu_ref[rows, cols] * y).astype(BF16)


def _inproj(x, pre_g, w_in, positions, inv_lanes, ln_g, ln_b, w_s, bs_full):
    seq, d_model = x.shape
    width = w_in.shape[1] // 5
    tm = 512
    row_spec = lambda cols: pl.BlockSpec((tm, cols), lambda i: (i, 0))
    const_spec = lambda shape, **kw: pl.BlockSpec(shape, lambda i: (0,) * len(shape), **kw)
    return pl.pallas_call(
        functools.partial(_inproj_kernel, q_scale=DIFF_HEAD_DIM ** -0.5 * math.log2(math.e)),
        grid=(seq // tm,),
        in_specs=[
            row_spec(d_model),
            const_spec((1, d_model)),
            const_spec(w_in.shape, pipeline_mode=pl.Buffered(1)),
            row_spec(1),
            const_spec((1, LANES)),
            const_spec((1, width)),
            const_spec((1, width)),
            const_spec(w_s.shape),
            const_spec(bs_full.shape),
        ],
        out_specs=[row_spec(width)] * 4,
        out_shape=[jax.ShapeDtypeStruct((seq, width), BF16)] * 4,
        scratch_shapes=[pltpu.VMEM((tm, d_model), BF16), pltpu.VMEM((tm, width), F32)],
        compiler_params=_params(("arbitrary",)),
        name="inproj",
    )(x, pre_g, w_in, positions, inv_lanes, ln_g, ln_b, w_s, bs_full)


def _attn_kernel(*refs, tkc, row_block, unroll, cast_blocks):
    n_cast = len(cast_blocks)
    lq1_ref, lk1_ref, lq2_ref, lk2_ref, q_ref, k_ref, v_ref, g_ref = refs[:8]
    cast_src = refs[8:8 + n_cast]
    o_ref = refs[8 + n_cast]
    cast_dst = refs[9 + n_cast:9 + 2 * n_cast]
    p_ref, shift_ref, lpart_ref, acc_ref, knorm_ref = refs[9 + 2 * n_cast:]

    step = pl.program_id(0) * pl.num_programs(1) + pl.program_id(1)
    for src, dst, n_blocks in zip(cast_src, cast_dst, cast_blocks):
        @pl.when(step < n_blocks)
        def _(src=src, dst=dst):
            dst[...] = src[...].astype(BF16)

    tq = q_ref.shape[0]
    nkc = k_ref.shape[0] // tkc
    lane = lax.broadcasted_iota(jnp.int32, (tq, LANES), 1)
    in_map1 = lane < DIFF_HEAD_DIM
    q = q_ref[...]
    zero = jnp.zeros_like(q)
    q1 = jnp.where(in_map1, q, zero)
    q2 = jnp.where(in_map1, zero, q)
    nt_dims = (((1,), (1,)), ((), ()))

    def key_rows(kc):
        return pl.ds(pl.multiple_of(kc * tkc, tkc), tkc)

    row_blocks = [slice(r, r + row_block) for r in range(0, tq, row_block)]
    lane_tiles = [slice(t, t + LANES) for t in range(0, tkc, LANES)]

    def sweep(body, unroll):
        lax.fori_loop(0, nkc, lambda kc, c: (body(kc), c)[1], 0, unroll=unroll)

    def map_sq_norms(x):
        sq = x.astype(F32)
        sq = sq * sq
        return (jnp.sum(jnp.where(in_map1[:x.shape[0]], sq, 0.0), axis=1, keepdims=True),
                jnp.sum(jnp.where(in_map1[:x.shape[0]], 0.0, sq), axis=1, keepdims=True))

    @pl.when(pl.program_id(1) == 0)
    def _():
        def chunk_max(kc, carry):
            n1, n2 = map_sq_norms(k_ref[pl.ds(pl.multiple_of(kc * tq, tq), tq), :])
            return (jnp.maximum(carry[0], jnp.max(n1, axis=0, keepdims=True)),
                    jnp.maximum(carry[1], jnp.max(n2, axis=0, keepdims=True)))

        zero11 = jnp.zeros((1, 1), F32)
        k1, k2 = lax.fori_loop(0, k_ref.shape[0] // tq, chunk_max, (zero11, zero11))
        knorm_ref[0] = jnp.broadcast_to(jnp.sqrt(k1), (8, LANES))
        knorm_ref[1] = jnp.broadcast_to(jnp.sqrt(k2), (8, LANES))

    lam = (jnp.exp(jnp.sum(lq1_ref[...] * lk1_ref[...], axis=1, keepdims=True))
           - jnp.exp(jnp.sum(lq2_ref[...] * lk2_ref[...], axis=1, keepdims=True)) + LAMBDA_INIT)

    def emit(out):
        o_ref[...] = (_rms(out, SUBLN_EPS) * g_ref[...] * (1.0 - LAMBDA_INIT)).astype(BF16)

    def row_sums():
        return (jnp.sum(lpart_ref[0], axis=1, keepdims=True),
                jnp.sum(lpart_ref[1], axis=1, keepdims=True))

    q_maps = jnp.concatenate([q1, q2], axis=0)

    def map_scores(kc):
        return lax.dot_general(q_maps, k_ref[key_rows(kc), :], nt_dims, preferred_element_type=F32)

    def map_tiles(update):
        for mp in range(2):
            for rows in row_blocks:
                update(mp, rows, slice(mp * tq + rows.start, mp * tq + rows.stop))

    def shifted_probs(kc):
        s = map_scores(kc)

        def update(mp, rows, s_rows):
            shift = shift_ref[mp, rows]
            l = lpart_ref[mp, rows]
            for t in lane_tiles:
                p = jnp.exp2(s[s_rows, t] - shift)
                p_ref[mp, kc, rows, t] = p.astype(BF16)
                l = l + p
            lpart_ref[mp, rows] = l

        map_tiles(update)

    qn1, qn2 = map_sq_norms(q)
    shift_ref[0] = jnp.sqrt(qn1) * knorm_ref[0, 0:1, :]
    shift_ref[1] = jnp.sqrt(qn2) * knorm_ref[1, 0:1, :]
    lpart_ref[...] = jnp.zeros_like(lpart_ref)
    sweep(shifted_probs, unroll)
    bounded_sums = row_sums()

    def redo_with_row_max():
        lpart_ref[...] = jnp.full(lpart_ref.shape, -jnp.inf, F32)

        def row_max(kc):
            s = map_scores(kc)

            def update(mp, rows, s_rows):
                m = lpart_ref[mp, rows]
                for t in lane_tiles:
                    m = jnp.maximum(m, s[s_rows, t])
                lpart_ref[mp, rows] = m

            map_tiles(update)

        sweep(row_max, 1)
        for mp in range(2):
            shift_ref[mp] = jnp.broadcast_to(jnp.max(lpart_ref[mp], axis=1, keepdims=True), (tq, LANES))
        lpart_ref[...] = jnp.zeros_like(lpart_ref)
        sweep(shifted_probs, 1)
        return row_sums()

    sums_ok = jnp.minimum(jnp.min(bounded_sums[0]), jnp.min(bounded_sums[1])) >= 2.0 ** -60
    l1, l2 = lax.cond(sums_ok, lambda: bounded_sums, redo_with_row_max)

    ratio = jnp.broadcast_to(lam * l1 / l2, (tq, LANES)).astype(BF16)
    acc_ref[...] = jnp.zeros_like(acc_ref)

    def combined_values(kc):
        w = jnp.concatenate([p_ref[0, kc, :, t] - p_ref[1, kc, :, t] * ratio for t in lane_tiles], axis=1)
        acc_ref[...] += jnp.dot(w, v_ref[key_rows(kc), :], preferred_element_type=F32)

    sweep(combined_values, unroll)
    emit(acc_ref[...] * (1.0 / l1))


BF16_SUBLANES = 16


def _cast_block_rows(rows, n_steps):
    for block in range(BF16_SUBLANES, rows + 1, BF16_SUBLANES):
        if rows % block == 0 and rows // block <= n_steps:
            return block
    raise ValueError(f"no row block for {rows} rows in {n_steps} steps")


def _attention(lq1, lk1, lq2, lk2, q, k, v, subln_g, f32_weights):
    seq = q.shape[0]
    tq = 1024
    tkc = 512
    n_q = seq // tq
    n_steps = N_DIFF_HEADS * n_q
    lam_spec = pl.BlockSpec((1, DIFF_HEAD_DIM), lambda h, i: (0, 0))
    cast_specs, cast_blocks = [], []
    for w in f32_weights:
        block = _cast_block_rows(w.shape[0], n_steps)
        n_blocks = w.shape[0] // block
        cast_blocks.append(n_blocks)
        cast_specs.append(pl.BlockSpec(
            (block, w.shape[1]), lambda h, i, last=n_blocks - 1: (jnp.minimum(h * n_q + i, last), 0)))
    outs = pl.pallas_call(
        functools.partial(_attn_kernel, tkc=tkc, row_block=64, unroll=8, cast_blocks=tuple(cast_blocks)),
        grid=(N_DIFF_HEADS, n_q),
        in_specs=[
            lam_spec, lam_spec, lam_spec, lam_spec,
            pl.BlockSpec((tq, DIFF_V_DIM), lambda h, i: (i, h)),
            pl.BlockSpec((seq, DIFF_V_DIM), lambda h, i: (0, h)),
            pl.BlockSpec((seq, DIFF_V_DIM), lambda h, i: (0, h)),
            pl.BlockSpec((1, DIFF_V_DIM), lambda h, i: (0, 0)),
        ] + cast_specs,
        out_specs=[pl.BlockSpec((tq, DIFF_V_DIM), lambda h, i: (i, h))] + cast_specs,
        out_shape=[jax.ShapeDtypeStruct((seq, N_DIFF_HEADS * DIFF_V_DIM), BF16)]
        + [jax.ShapeDtypeStruct(w.shape, BF16) for w in f32_weights],
        scratch_shapes=[pltpu.VMEM((2, seq // tkc, tq, tkc), BF16),
                        pltpu.VMEM((2, tq, LANES), F32),
                        pltpu.VMEM((2, tq, LANES), F32),
                        pltpu.VMEM((tq, LANES), F32),
                        pltpu.VMEM((2, 8, LANES), F32)],
        compiler_params=_params(("arbitrary", "arbitrary")),
        name="diff_attention",
    )(lq1, lk1, lq2, lk2, q, k, v, subln_g, *f32_weights)
    return outs[0], outs[1:]


def _outproj_kernel(a_ref, gm_ref, w_ref, x_ref, gpost_ref, gffn_ref, x1_ref, h2_ref, *, row_block):
    tm, half = a_ref.shape
    for r in range(0, tm, row_block):
        rows = slice(r, r + row_block)
        mix = (jnp.dot(a_ref[rows, :], w_ref[0:half, :], preferred_element_type=F32)
               + jnp.dot(gm_ref[rows, :], w_ref[half:2 * half, :], preferred_element_type=F32))
        x1 = x_ref[rows, :] + _rms(mix, RMS_EPS) * gpost_ref[...]
        x1_ref[rows, :] = x1
        h2_ref[rows, :] = (_rms(x1, RMS_EPS) * gffn_ref[...]).astype(BF16)


def _outproj(attn, gm, w_out, x, post_mix_g, pre_ffn_g):
    seq, d_model = x.shape
    half = attn.shape[1]
    tm = 512
    row_spec = lambda cols: pl.BlockSpec((tm, cols), lambda i: (i, 0))
    const_spec = lambda shape: pl.BlockSpec(shape, lambda i: (0, 0))
    return pl.pallas_call(
        functools.partial(_outproj_kernel, row_block=128),
        grid=(seq // tm,),
        in_specs=[row_spec(half), row_spec(half), const_spec(w_out.shape), row_spec(d_model),
                  const_spec((1, d_model)), const_spec((1, d_model))],
        out_specs=[row_spec(d_model), row_spec(d_model)],
        out_shape=[jax.ShapeDtypeStruct((seq, d_model), F32), jax.ShapeDtypeStruct((seq, d_model), BF16)],
        compiler_params=_params(("arbitrary",)),
        name="outproj",
    )(attn, gm, w_out, x, post_mix_g, pre_ffn_g)


def _ffn_kernel(h_ref, wg_ref, wu_ref, wd_ref, x1_ref, g_ref, o_ref, acc_ref):
    j = pl.program_id(1)

    @pl.when(j == 0)
    def _():
        acc_ref[...] = jnp.zeros_like(acc_ref)

    h = h_ref[...]
    gate = jnp.dot(h, wg_ref[...], preferred_element_type=F32)
    up = jnp.dot(h, wu_ref[...], preferred_element_type=F32)
    act = (gate * jax.nn.sigmoid(gate) * up).astype(BF16)
    acc_ref[...] += jnp.dot(act, wd_ref[...], preferred_element_type=F32)

    @pl.when(j == pl.num_programs(1) - 1)
    def _():
        o_ref[...] = x1_ref[...] + _rms(acc_ref[...], RMS_EPS) * g_ref[...]


def _ffn(h2, w_gate, w_up, w_down, x1, post_ffn_g):
    seq, d_model = x1.shape
    d_ff = w_gate.shape[1]
    tm = 512
    tf = 512
    return pl.pallas_call(
        _ffn_kernel,
        grid=(seq // tm, d_ff // tf),
        in_specs=[
            pl.BlockSpec((tm, d_model), lambda i, j: (i, 0)),
            pl.BlockSpec((d_model, tf), lambda i, j: (0, j)),
            pl.BlockSpec((d_model, tf), lambda i, j: (0, j)),
            pl.BlockSpec((tf, d_model), lambda i, j: (j, 0)),
            pl.BlockSpec((tm, d_model), lambda i, j: (i, 0)),
            pl.BlockSpec((1, d_model), lambda i, j: (0, 0)),
        ],
        out_specs=pl.BlockSpec((tm, d_model), lambda i, j: (i, 0)),
        out_shape=jax.ShapeDtypeStruct((seq, d_model), F32),
        scratch_shapes=[pltpu.VMEM((tm, d_model), F32)],
        compiler_params=_params(("arbitrary", "arbitrary")),
        name="ffn",
    )(h2, w_gate, w_up, w_down, x1, post_ffn_g)


def kernel(x, positions, pre_mix_g, w_in, lambda_q1, lambda_k1, lambda_q2, lambda_k2, subln_g,
           gmlp_ln_g, gmlp_ln_b, w_s, b_s, w_out, post_mix_g, pre_ffn_g, w_gate, w_up, w_down,
           post_ffn_g):
    batch, seq, d_model = x.shape
    assert batch == 1 and pre_mix_g.shape[0] == 1, "single sequence, single layer"
    assert seq % 1024 == 0 and d_model == 2048
    x2 = x[0]
    inv_freq = ROPE_THETA ** (-jnp.arange(0, DIFF_HEAD_DIM, 2, dtype=F32) / DIFF_HEAD_DIM)
    inv_lanes = jnp.tile(inv_freq, LANES // (DIFF_HEAD_DIM // 2))[None, :]
    bs_full = jnp.repeat(b_s[0].T, LANES, axis=1)
    q, k, v, gm = _inproj(x2, pre_mix_g, w_in[0].astype(BF16), positions[0][:, None], inv_lanes,
                          gmlp_ln_g, gmlp_ln_b, w_s[0].astype(BF16), bs_full)
    attn, (w_out16, w_gate16, w_up16, w_down16) = _attention(
        lambda_q1, lambda_k1, lambda_q2, lambda_k2, q, k, v, subln_g,
        (w_out[0], w_gate[0], w_up[0], w_down[0]))
    x1, h2 = _outproj(attn, gm, w_out16, x2, post_mix_g, pre_ffn_g)
    out = _ffn(h2, w_gate16, w_up16, w_down16, x1, post_ffn_g)
    return out[None]
```

```python
import functools
import math

import jax
import jax.numpy as jnp
from jax import lax
from jax.experimental import pallas as pl
from jax.experimental.pallas import tpu as pltpu

N_DIFF_HEADS = 8
DIFF_HEAD_DIM = 64
DIFF_V_DIM = 2 * DIFF_HEAD_DIM
N_GMLP_GROUPS = 8
CHUNK = 128
ROPE_THETA = 10000.0
RMS_EPS = 1e-6
LN_EPS = 1e-5
SUBLN_EPS = 1e-5
LAMBDA_INIT = 0.8 - 0.6 * math.exp(-0.3 * 0)

LANES = 128
VMEM_LIMIT_BYTES = 56 * 1024 * 1024

F32 = jnp.float32
BF16 = jnp.bfloat16


def _rms(x, eps):
    return x * lax.rsqrt(jnp.mean(x * x, axis=-1, keepdims=True) + eps)


def _gelu(x):
    return 0.5 * x * (1.0 + lax.erf(x * (1.0 / math.sqrt(2.0))))


def _params(semantics):
    return pltpu.CompilerParams(dimension_semantics=semantics, vmem_limit_bytes=VMEM_LIMIT_BYTES)


def _first_half_lanes(shape):
    lane = lax.broadcasted_iota(jnp.int32, shape, 1)
    return (lane % DIFF_HEAD_DIM) < (DIFF_HEAD_DIM // 2)


def _rope(x, cos, sin_signed):
    half = DIFF_HEAD_DIM // 2
    partner = jnp.where(_first_half_lanes(x.shape), pltpu.roll(x, LANES - half, 1), pltpu.roll(x, half, 1))
    return x * cos + partner * sin_signed


def _inproj_kernel(x_ref, g_ref, w_ref, pos_ref, inv_ref, lng_ref, lnb_ref, ws_ref, bs_ref,
                   q_ref, k_ref, v_ref, gm_ref, h_ref, gu_ref, *, q_scale):
    tm, width = q_ref.shape
    q_cols, k_cols, v_cols, gu_cols, gv_cols = (slice(n * width, (n + 1) * width) for n in range(5))

    def proj(cols):
        return jnp.dot(h_ref[...], w_ref[:, cols], preferred_element_type=F32)

    h_ref[...] = (_rms(x_ref[...], RMS_EPS) * g_ref[...]).astype(BF16)
    v_ref[...] = proj(v_cols).astype(BF16)

    ang = pos_ref[...].astype(F32) * inv_ref[...]
    sin = jnp.sin(ang)
    sin = jnp.where(_first_half_lanes(ang.shape), -sin, sin)
    cos = jnp.cos(ang)

    def rope_store(out_ref, cols, scale):
        acc = proj(cols)
        for c in range(width // LANES):
            sl = slice(c * LANES, (c + 1) * LANES)
            out_ref[:, sl] = (_rope(acc[:, sl], cos, sin) * scale).astype(BF16)

    rope_store(q_ref, q_cols, q_scale)
    rope_store(k_ref, k_cols, 1.0)

    gu_ref[...] = _gelu(proj(gu_cols))
    gv = _gelu(proj(gv_cols))
    mu = jnp.mean(gv, axis=-1, keepdims=True)
    d = gv - mu
    var = jnp.mean(d * d, axis=-1, keepdims=True)
    vln = (d * lax.rsqrt(var + LN_EPS) * lng_ref[...] + lnb_ref[...]).astype(BF16)
    for c in range(tm // CHUNK):
        rows = slice(c * CHUNK, (c + 1) * CHUNK)
        for g in range(N_GMLP_GROUPS):
            cols = slice(g * LANES, (g + 1) * LANES)
            y = jnp.dot(ws_ref[g], vln[rows, cols], preferred_element_type=F32) + bs_ref[:, cols]
            gm_ref[rows, cols] = (gu_ref[rows, cols] * y).astype(BF16)


def _inproj(x, pre_g, w_in, positions, inv_lanes, ln_g, ln_b, w_s, bs_full):
    seq, d_model = x.shape
    width = w_in.shape[1] // 5
    tm = 512
    row_spec = lambda cols: pl.BlockSpec((tm, cols), lambda i: (i, 0))
    const_spec = lambda shape, **kw: pl.BlockSpec(shape, lambda i: (0,) * len(shape), **kw)
    return pl.pallas_call(
        functools.partial(_inproj_kernel, q_scale=DIFF_HEAD_DIM ** -0.5 * math.log2(math.e)),
        grid=(seq // tm,),
        in_specs=[
            row_spec(d_model),
            const_spec((1, d_model)),
            const_spec(w_in.shape, pipeline_mode=pl.Buffered(1)),
            row_spec(1),
            const_spec((1, LANES)),
            const_spec((1, width)),
            const_spec((1, width)),
            const_spec(w_s.shape),
            const_spec(bs_full.shape),
        ],
        out_specs=[row_spec(width)] * 4,
        out_shape=[jax.ShapeDtypeStruct((seq, width), BF16)] * 4,
        scratch_shapes=[pltpu.VMEM((tm, d_model), BF16), pltpu.VMEM((tm, width), F32)],
        compiler_params=_params(("arbitrary",)),
        name="inproj",
    )(x, pre_g, w_in, positions, inv_lanes, ln_g, ln_b, w_s, bs_full)


def _attn_kernel(*refs, tkc, row_block, unroll, cast_blocks):
    n_cast = len(cast_blocks)
    lq1_ref, lk1_ref, lq2_ref, lk2_ref, q_ref, k_ref, v_ref, g_ref = refs[:8]
    cast_src = refs[8:8 + n_cast]
    o_ref = refs[8 + n_cast]
    cast_dst = refs[9 + n_cast:9 + 2 * n_cast]
    p_ref, shift_ref, lpart_ref, acc_ref, knorm_ref = refs[9 + 2 * n_cast:]

    step = pl.program_id(0) * pl.num_programs(1) + pl.program_id(1)
    for src, dst, n_blocks in zip(cast_src, cast_dst, cast_blocks):
        @pl.when(step < n_blocks)
        def _(src=src, dst=dst):
            dst[...] = src[...].astype(BF16)

    tq = q_ref.shape[0]
    nkc = k_ref.shape[0] // tkc
    lane = lax.broadcasted_iota(jnp.int32, (tq, LANES), 1)
    in_map1 = lane < DIFF_HEAD_DIM
    q = q_ref[...]
    zero = jnp.zeros_like(q)
    q1 = jnp.where(in_map1, q, zero)
    q2 = jnp.where(in_map1, zero, q)
    nt_dims = (((1,), (1,)), ((), ()))

    def key_rows(kc):
        return pl.ds(pl.multiple_of(kc * tkc, tkc), tkc)

    row_blocks = [slice(r, r + row_block) for r in range(0, tq, row_block)]
    lane_tiles = [slice(t, t + LANES) for t in range(0, tkc, LANES)]

    def sweep(body, unroll):
        lax.fori_loop(0, nkc, lambda kc, c: (body(kc), c)[1], 0, unroll=unroll)

    def map_sq_norms(x):
        sq = x.astype(F32)
        sq = sq * sq
        return (jnp.sum(jnp.where(in_map1[:x.shape[0]], sq, 0.0), axis=1, keepdims=True),
                jnp.sum(jnp.where(in_map1[:x.shape[0]], 0.0, sq), axis=1, keepdims=True))

    @pl.when(pl.program_id(1) == 0)
    def _():
        def chunk_max(kc, carry):
            n1, n2 = map_sq_norms(k_ref[pl.ds(pl.multiple_of(kc * tq, tq), tq), :])
            return (jnp.maximum(carry[0], jnp.max(n1, axis=0, keepdims=True)),
                    jnp.maximum(carry[1], jnp.max(n2, axis=0, keepdims=True)))

        zero11 = jnp.zeros((1, 1), F32)
        k1, k2 = lax.fori_loop(0, k_ref.shape[0] // tq, chunk_max, (zero11, zero11))
        knorm_ref[0] = jnp.broadcast_to(jnp.sqrt(k1), (8, LANES))
        knorm_ref[1] = jnp.broadcast_to(jnp.sqrt(k2), (8, LANES))

    lam = (jnp.exp(jnp.sum(lq1_ref[...] * lk1_ref[...], axis=1, keepdims=True))
           - jnp.exp(jnp.sum(lq2_ref[...] * lk2_ref[...], axis=1, keepdims=True)) + LAMBDA_INIT)

    def emit(out):
        o_ref[...] = (_rms(out, SUBLN_EPS) * g_ref[...] * (1.0 - LAMBDA_INIT)).astype(BF16)

    def row_sums():
        return (jnp.sum(lpart_ref[0], axis=1, keepdims=True),
                jnp.sum(lpart_ref[1], axis=1, keepdims=True))

    q_maps = jnp.concatenate([q1, q2], axis=0)

    def map_scores(kc):
        return lax.dot_general(q_maps, k_ref[key_rows(kc), :], nt_dims, preferred_element_type=F32)

    def map_tiles(update):
        for mp in range(2):
            for rows in row_blocks:
                update(mp, rows, slice(mp * tq + rows.start, mp * tq + rows.stop))

    def shifted_probs(kc):
        s = map_scores(kc)

        def update(mp, rows, s_rows):
            shift = shift_ref[mp, rows]
            l = lpart_ref[mp, rows]
            for t in lane_tiles:
                p = jnp.exp2(s[s_rows, t] - shift)
                p_ref[mp, kc, rows, t] = p.astype(BF16)
                l = l + p
            lpart_ref[mp, rows] = l

        map_tiles(update)

    qn1, qn2 = map_sq_norms(q)
    shift_ref[0] = jnp.sqrt(qn1) * knorm_ref[0, 0:1, :]
    shift_ref[1] = jnp.sqrt(qn2) * knorm_ref[1, 0:1, :]
    lpart_ref[...] = jnp.zeros_like(lpart_ref)
    sweep(shifted_probs, unroll)
    bounded_sums = row_sums()

    def redo_with_row_max():
        lpart_ref[...] = jnp.full(lpart_ref.shape, -jnp.inf, F32)

        def row_max(kc):
            s = map_scores(kc)

            def update(mp, rows, s_rows):
                m = lpart_ref[mp, rows]
                for t in lane_tiles:
                    m = jnp.maximum(m, s[s_rows, t])
                lpart_ref[mp, rows] = m

            map_tiles(update)

        sweep(row_max, 1)
        for mp in range(2):
            shift_ref[mp] = jnp.broadcast_to(jnp.max(lpart_ref[mp], axis=1, keepdims=True), (tq, LANES))
        lpart_ref[...] = jnp.zeros_like(lpart_ref)
        sweep(shifted_probs, 1)
        return row_sums()

    sums_ok = jnp.minimum(jnp.min(bounded_sums[0]), jnp.min(bounded_sums[1])) >= 2.0 ** -60
    l1, l2 = lax.cond(sums_ok, lambda: bounded_sums, redo_with_row_max)

    ratio = jnp.broadcast_to(lam * l1 / l2, (tq, LANES)).astype(BF16)
    acc_ref[...] = jnp.zeros_like(acc_ref)

    def combined_values(kc):
        w = jnp.concatenate([p_ref[0, kc, :, t] - p_ref[1, kc, :, t] * ratio for t in lane_tiles], axis=1)
        acc_ref[...] += jnp.dot(w, v_ref[key_rows(kc), :], preferred_element_type=F32)

    sweep(combined_values, unroll)
    emit(acc_ref[...] * (1.0 / l1))


BF16_SUBLANES = 16


def _cast_block_rows(rows, n_steps):
    for block in range(BF16_SUBLANES, rows + 1, BF16_SUBLANES):
        if rows % block == 0 and rows // block <= n_steps:
            return block
    raise ValueError(f"no row block for {rows} rows in {n_steps} steps")


def _attention(lq1, lk1, lq2, lk2, q, k, v, subln_g, f32_weights):
    seq = q.shape[0]
    tq = 1024
    tkc = 512
    n_q = seq // tq
    n_steps = N_DIFF_HEADS * n_q
    lam_spec = pl.BlockSpec((1, DIFF_HEAD_DIM), lambda h, i: (0, 0))
    cast_specs, cast_blocks = [], []
    for w in f32_weights:
        block = _cast_block_rows(w.shape[0], n_steps)
        n_blocks = w.shape[0] // block
        cast_blocks.append(n_blocks)
        cast_specs.append(pl.BlockSpec(
            (block, w.shape[1]), lambda h, i, last=n_blocks - 1: (jnp.minimum(h * n_q + i, last), 0)))
    outs = pl.pallas_call(
        functools.partial(_attn_kernel, tkc=tkc, row_block=64, unroll=16, cast_blocks=tuple(cast_blocks)),
        grid=(N_DIFF_HEADS, n_q),
        in_specs=[
            lam_spec, lam_spec, lam_spec, lam_spec,
            pl.BlockSpec((tq, DIFF_V_DIM), lambda h, i: (i, h)),
            pl.BlockSpec((seq, DIFF_V_DIM), lambda h, i: (0, h)),
            pl.BlockSpec((seq, DIFF_V_DIM), lambda h, i: (0, h)),
            pl.BlockSpec((1, DIFF_V_DIM), lambda h, i: (0, 0)),
        ] + cast_specs,
        out_specs=[pl.BlockSpec((tq, DIFF_V_DIM), lambda h, i: (i, h))] + cast_specs,
        out_shape=[jax.ShapeDtypeStruct((seq, N_DIFF_HEADS * DIFF_V_DIM), BF16)]
        + [jax.ShapeDtypeStruct(w.shape, BF16) for w in f32_weights],
        scratch_shapes=[pltpu.VMEM((2, seq // tkc, tq, tkc), BF16),
                        pltpu.VMEM((2, tq, LANES), F32),
                        pltpu.VMEM((2, tq, LANES), F32),
                        pltpu.VMEM((tq, LANES), F32),
                        pltpu.VMEM((2, 8, LANES), F32)],
        compiler_params=_params(("arbitrary", "arbitrary")),
        name="diff_attention",
    )(lq1, lk1, lq2, lk2, q, k, v, subln_g, *f32_weights)
    return outs[0], outs[1:]


def _outproj_kernel(a_ref, gm_ref, w_ref, x_ref, gpost_ref, gffn_ref, x1_ref, h2_ref, *, row_block):
    tm, half = a_ref.shape
    for r in range(0, tm, row_block):
        rows = slice(r, r + row_block)
        mix = (jnp.dot(a_ref[rows, :], w_ref[0:half, :], preferred_element_type=F32)
               + jnp.dot(gm_ref[rows, :], w_ref[half:2 * half, :], preferred_element_type=F32))
        x1 = x_ref[rows, :] + _rms(mix, RMS_EPS) * gpost_ref[...]
        x1_ref[rows, :] = x1
        h2_ref[rows, :] = (_rms(x1, RMS_EPS) * gffn_ref[...]).astype(BF16)


def _outproj(attn, gm, w_out, x, post_mix_g, pre_ffn_g):
    seq, d_model = x.shape
    half = attn.shape[1]
    tm = 512
    row_spec = lambda cols: pl.BlockSpec((tm, cols), lambda i: (i, 0))
    const_spec = lambda shape: pl.BlockSpec(shape, lambda i: (0, 0))
    return pl.pallas_call(
        functools.partial(_outproj_kernel, row_block=128),
        grid=(seq // tm,),
        in_specs=[row_spec(half), row_spec(half), const_spec(w_out.shape), row_spec(d_model),
                  const_spec((1, d_model)), const_spec((1, d_model))],
        out_specs=[row_spec(d_model), row_spec(d_model)],
        out_shape=[jax.ShapeDtypeStruct((seq, d_model), F32), jax.ShapeDtypeStruct((seq, d_model), BF16)],
        compiler_params=_params(("arbitrary",)),
        name="outproj",
    )(attn, gm, w_out, x, post_mix_g, pre_ffn_g)


def _ffn_kernel(h_ref, wg_ref, wu_ref, wd_ref, x1_ref, g_ref, o_ref, acc_ref):
    j = pl.program_id(1)

    @pl.when(j == 0)
    def _():
        acc_ref[...] = jnp.zeros_like(acc_ref)

    h = h_ref[...]
    gate = jnp.dot(h, wg_ref[...], preferred_element_type=F32)
    up = jnp.dot(h, wu_ref[...], preferred_element_type=F32)
    act = (gate * jax.nn.sigmoid(gate) * up).astype(BF16)
    acc_ref[...] += jnp.dot(act, wd_ref[...], preferred_element_type=F32)

    @pl.when(j == pl.num_programs(1) - 1)
    def _():
        o_ref[...] = x1_ref[...] + _rms(acc_ref[...], RMS_EPS) * g_ref[...]


def _ffn(h2, w_gate, w_up, w_down, x1, post_ffn_g):
    seq, d_model = x1.shape
    d_ff = w_gate.shape[1]
    tm = 512
    tf = 512
    return pl.pallas_call(
        _ffn_kernel,
        grid=(seq // tm, d_ff // tf),
        in_specs=[
            pl.BlockSpec((tm, d_model), lambda i, j: (i, 0)),
            pl.BlockSpec((d_model, tf), lambda i, j: (0, j)),
            pl.BlockSpec((d_model, tf), lambda i, j: (0, j)),
            pl.BlockSpec((tf, d_model), lambda i, j: (j, 0)),
            pl.BlockSpec((tm, d_model), lambda i, j: (i, 0)),
            pl.BlockSpec((1, d_model), lambda i, j: (0, 0)),
        ],
        out_specs=pl.BlockSpec((tm, d_model), lambda i, j: (i, 0)),
        out_shape=jax.ShapeDtypeStruct((seq, d_model), F32),
        scratch_shapes=[pltpu.VMEM((tm, d_model), F32)],
        compiler_params=_params(("arbitrary", "arbitrary")),
        name="ffn",
    )(h2, w_gate, w_up, w_down, x1, post_ffn_g)


def kernel(x, positions, pre_mix_g, w_in, lambda_q1, lambda_k1, lambda_q2, lambda_k2, subln_g,
           gmlp_ln_g, gmlp_ln_b, w_s, b_s, w_out, post_mix_g, pre_ffn_g, w_gate, w_up, w_down,
           post_ffn_g):
    batch, seq, d_model = x.shape
    assert batch == 1 and pre_mix_g.shape[0] == 1, "single sequence, single layer"
    assert seq % 1024 == 0 and d_model == 2048
    x2 = x[0]
    inv_freq = ROPE_THETA ** (-jnp.arange(0, DIFF_HEAD_DIM, 2, dtype=F32) / DIFF_HEAD_DIM)
    inv_lanes = jnp.tile(inv_freq, LANES // (DIFF_HEAD_DIM // 2))[None, :]
    bs_full = jnp.repeat(b_s[0].T, LANES, axis=1)
    q, k, v, gm = _inproj(x2, pre_mix_g, w_in[0].astype(BF16), positions[0][:, None], inv_lanes,
                          gmlp_ln_g, gmlp_ln_b, w_s[0].astype(BF16), bs_full)
    attn, (w_out16, w_gate16, w_up16, w_down16) = _attention(
        lambda_q1, lambda_k1, lambda_q2, lambda_k2, q, k, v, subln_g,
        (w_out[0], w_gate[0], w_up[0], w_down[0]))
    x1, h2 = _outproj(attn, gm, w_out16, x2, post_mix_g, pre_ffn_g)
    out = _ffn(h2, w_gate16, w_up16, w_down16, x1, post_ffn_g)
    return out[None]
```

```python
import functools
import math

import jax
import jax.numpy as jnp
from jax import lax
from jax.experimental import pallas as pl
from jax.experimental.pallas import tpu as pltpu

N_DIFF_HEADS = 8
DIFF_HEAD_DIM = 64
DIFF_V_DIM = 2 * DIFF_HEAD_DIM
N_GMLP_GROUPS = 8
CHUNK = 128
ROPE_THETA = 10000.0
RMS_EPS = 1e-6
LN_EPS = 1e-5
SUBLN_EPS = 1e-5
LAMBDA_INIT = 0.8 - 0.6 * math.exp(-0.3 * 0)

LANES = 128
VMEM_LIMIT_BYTES = 56 * 1024 * 1024

F32 = jnp.float32
BF16 = jnp.bfloat16


def _rms(x, eps):
    return x * lax.rsqrt(jnp.mean(x * x, axis=-1, keepdims=True) + eps)


def _gelu(x):
    return 0.5 * x * (1.0 + lax.erf(x * (1.0 / math.sqrt(2.0))))


def _params(semantics):
    return pltpu.CompilerParams(dimension_semantics=semantics, vmem_limit_bytes=VMEM_LIMIT_BYTES)


def _first_half_lanes(shape):
    lane = lax.broadcasted_iota(jnp.int32, shape, 1)
    return (lane % DIFF_HEAD_DIM) < (DIFF_HEAD_DIM // 2)


def _rope(x, cos, sin_signed):
    half = DIFF_HEAD_DIM // 2
    partner = jnp.where(_first_half_lanes(x.shape), pltpu.roll(x, LANES - half, 1), pltpu.roll(x, half, 1))
    return x * cos + partner * sin_signed


def _inproj_kernel(x_ref, g_ref, w_ref, pos_ref, inv_ref, lng_ref, lnb_ref, ws_ref, bs_ref,
                   q_ref, k_ref, v_ref, gm_ref, nmax_ref, h_ref, gu_ref, *, q_scale):
    tm, width = q_ref.shape
    q_cols, k_cols, v_cols, gu_cols, gv_cols = (slice(n * width, (n + 1) * width) for n in range(5))

    def proj(cols):
        return jnp.dot(h_ref[...], w_ref[:, cols], preferred_element_type=F32)

    h_ref[...] = (_rms(x_ref[...], RMS_EPS) * g_ref[...]).astype(BF16)
    v_ref[...] = proj(v_cols).astype(BF16)

    ang = pos_ref[...].astype(F32) * inv_ref[...]
    sin = jnp.sin(ang)
    sin = jnp.where(_first_half_lanes(ang.shape), -sin, sin)
    cos = jnp.cos(ang)

    map1_lanes = lax.broadcasted_iota(jnp.int32, (tm, LANES), 1) < DIFF_HEAD_DIM
    nmax_lane = lax.broadcasted_iota(jnp.int32, nmax_ref.shape[1:], 1)

    def rope_store(out_ref, cols, scale, nmax, first_lane):
        acc = proj(cols)
        for c in range(width // LANES):
            sl = slice(c * LANES, (c + 1) * LANES)
            rotated = (_rope(acc[:, sl], cos, sin) * scale).astype(BF16)
            out_ref[:, sl] = rotated
            sq = rotated.astype(F32)
            sq = sq * sq
            for m, in_map in enumerate((map1_lanes, jnp.logical_not(map1_lanes))):
                norm_sq = jnp.sum(jnp.where(in_map, sq, 0.0), axis=1, keepdims=True)
                nmax = jnp.where(nmax_lane == first_lane + 2 * c + m,
                                 jnp.max(norm_sq, axis=0, keepdims=True), nmax)
        return nmax

    nmax = rope_store(q_ref, q_cols, q_scale, jnp.zeros(nmax_ref.shape[1:], F32), 0)
    nmax_ref[0] = rope_store(k_ref, k_cols, 1.0, nmax, 2 * N_DIFF_HEADS)

    gu_ref[...] = _gelu(proj(gu_cols))
    gv = _gelu(proj(gv_cols))
    mu = jnp.mean(gv, axis=-1, keepdims=True)
    d = gv - mu
    var = jnp.mean(d * d, axis=-1, keepdims=True)
    vln = (d * lax.rsqrt(var + LN_EPS) * lng_ref[...] + lnb_ref[...]).astype(BF16)
    for c in range(tm // CHUNK):
        rows = slice(c * CHUNK, (c + 1) * CHUNK)
        for g in range(N_GMLP_GROUPS):
            cols = slice(g * LANES, (g + 1) * LANES)
            y = jnp.dot(ws_ref[g], vln[rows, cols], preferred_element_type=F32) + bs_ref[:, cols]
            gm_ref[rows, cols] = (gu_ref[rows, cols] * y).astype(BF16)


def _inproj(x, pre_g, w_in, positions, inv_lanes, ln_g, ln_b, w_s, bs_full):
    seq, d_model = x.shape
    width = w_in.shape[1] // 5
    tm = 512
    row_spec = lambda cols: pl.BlockSpec((tm, cols), lambda i: (i, 0))
    const_spec = lambda shape, **kw: pl.BlockSpec(shape, lambda i: (0,) * len(shape), **kw)
    return pl.pallas_call(
        functools.partial(_inproj_kernel, q_scale=DIFF_HEAD_DIM ** -0.5 * math.log2(math.e)),
        grid=(seq // tm,),
        in_specs=[
            row_spec(d_model),
            const_spec((1, d_model)),
            const_spec(w_in.shape, pipeline_mode=pl.Buffered(1)),
            row_spec(1),
            const_spec((1, LANES)),
            const_spec((1, width)),
            const_spec((1, width)),
            const_spec(w_s.shape),
            const_spec(bs_full.shape),
        ],
        out_specs=[row_spec(width)] * 4 + [pl.BlockSpec((1, 8, LANES), lambda i: (i, 0, 0))],
        out_shape=[jax.ShapeDtypeStruct((seq, width), BF16)] * 4
        + [jax.ShapeDtypeStruct((seq // tm, 8, LANES), F32)],
        scratch_shapes=[pltpu.VMEM((tm, d_model), BF16), pltpu.VMEM((tm, width), F32)],
        compiler_params=_params(("arbitrary",)),
        name="inproj",
    )(x, pre_g, w_in, positions, inv_lanes, ln_g, ln_b, w_s, bs_full)


def _attn_kernel(*refs, tkc, row_block, unroll, cast_blocks):
    n_cast = len(cast_blocks)
    lq1_ref, lk1_ref, lq2_ref, lk2_ref, q_ref, k_ref, v_ref, g_ref, nmax_ref = refs[:9]
    cast_src = refs[9:9 + n_cast]
    o_ref = refs[9 + n_cast]
    cast_dst = refs[10 + n_cast:10 + 2 * n_cast]
    p_ref, shift_ref, lpart_ref, acc_ref = refs[10 + 2 * n_cast:]

    step = pl.program_id(0) * pl.num_programs(1) + pl.program_id(1)
    for src, dst, n_blocks in zip(cast_src, cast_dst, cast_blocks):
        @pl.when(step < n_blocks)
        def _(src=src, dst=dst):
            dst[...] = src[...].astype(BF16)

    tq = q_ref.shape[0]
    nkc = k_ref.shape[0] // tkc
    lane = lax.broadcasted_iota(jnp.int32, (tq, LANES), 1)
    in_map1 = lane < DIFF_HEAD_DIM
    q = q_ref[...]
    zero = jnp.zeros_like(q)
    q1 = jnp.where(in_map1, q, zero)
    q2 = jnp.where(in_map1, zero, q)
    nt_dims = (((1,), (1,)), ((), ()))

    def key_rows(kc):
        return pl.ds(pl.multiple_of(kc * tkc, tkc), tkc)

    row_blocks = [slice(r, r + row_block) for r in range(0, tq, row_block)]
    lane_tiles = [slice(t, t + LANES) for t in range(0, tkc, LANES)]

    def sweep(body, unroll):
        lax.fori_loop(0, nkc, lambda kc, c: (body(kc), c)[1], 0, unroll=unroll)

    norm_sq_max = jnp.max(nmax_ref[...], axis=0)
    nmax_lane = lax.broadcasted_iota(jnp.int32, norm_sq_max.shape, 1)
    head_lane = 2 * pl.program_id(0)

    def norm_sq(lane_index):
        picked = jnp.where(nmax_lane == lane_index, norm_sq_max, 0.0)
        return jnp.max(jnp.max(picked, axis=1, keepdims=True), axis=0, keepdims=True)

    score_bound = [jnp.sqrt(norm_sq(head_lane + mp) * norm_sq(2 * N_DIFF_HEADS + head_lane + mp))
                   for mp in range(2)]

    lam =(jnp.exp(jnp.sum(lq1_ref[...] * lk1_ref[...], axis=1, keepdims=True))
           - jnp.exp(jnp.sum(lq2_ref[...] * lk2_ref[...], axis=1, keepdims=True)) + LAMBDA_INIT)

    def emit(out):
        o_ref[...] = (_rms(out, SUBLN_EPS) * g_ref[...] * (1.0 - LAMBDA_INIT)).astype(BF16)

    def row_sums():
        return (jnp.sum(lpart_ref[0], axis=1, keepdims=True),
                jnp.sum(lpart_ref[1], axis=1, keepdims=True))

    q_maps = jnp.concatenate([q1, q2], axis=0)

    def map_scores(kc):
        return lax.dot_general(q_maps, k_ref[key_rows(kc), :], nt_dims, preferred_element_type=F32)

    def map_tiles(update):
        for mp in range(2):
            for rows in row_blocks:
                update(mp, rows, slice(mp * tq + rows.start, mp * tq + rows.stop))

    def shifted_probs(kc):
        s = map_scores(kc)

        def update(mp, rows, s_rows):
            shift = shift_ref[mp, rows]
            l = lpart_ref[mp, rows]
            for t in lane_tiles:
                p = jnp.exp2(s[s_rows, t] - shift)
                p_ref[mp, kc, rows, t] = p.astype(BF16)
                l = l + p
            lpart_ref[mp, rows] = l

        map_tiles(update)

    for mp in range(2):
        shift_ref[mp] = jnp.broadcast_to(score_bound[mp], (tq, LANES))
    lpart_ref[...] = jnp.zeros_like(lpart_ref)
    sweep(shifted_probs, unroll)
    bounded_sums = row_sums()

    def redo_with_row_max():
        lpart_ref[...] = jnp.full(lpart_ref.shape, -jnp.inf, F32)

        def row_max(kc):
            s = map_scores(kc)

            def update(mp, rows, s_rows):
                m = lpart_ref[mp, rows]
                for t in lane_tiles:
                    m = jnp.maximum(m, s[s_rows, t])
                lpart_ref[mp, rows] = m

            map_tiles(update)

        sweep(row_max, 1)
        for mp in range(2):
            shift_ref[mp] = jnp.broadcast_to(jnp.max(lpart_ref[mp], axis=1, keepdims=True), (tq, LANES))
        lpart_ref[...] = jnp.zeros_like(lpart_ref)
        sweep(shifted_probs, 1)
        return row_sums()

    sums_ok = jnp.minimum(jnp.min(bounded_sums[0]), jnp.min(bounded_sums[1])) >= 2.0 ** -60
    l1, l2 = lax.cond(sums_ok, lambda: bounded_sums, redo_with_row_max)

    ratio = jnp.broadcast_to(lam * l1 / l2, (tq, LANES)).astype(BF16)
    acc_ref[...] = jnp.zeros_like(acc_ref)

    def combined_values(kc):
        w = jnp.concatenate([p_ref[0, kc, :, t] - p_ref[1, kc, :, t] * ratio for t in lane_tiles], axis=1)
        acc_ref[...] += jnp.dot(w, v_ref[key_rows(kc), :], preferred_element_type=F32)

    sweep(combined_values, unroll)
    emit(acc_ref[...] * (1.0 / l1))


BF16_SUBLANES = 16


def _cast_block_rows(rows, n_steps):
    for block in range(BF16_SUBLANES, rows + 1, BF16_SUBLANES):
        if rows % block == 0 and rows // block <= n_steps:
            return block
    raise ValueError(f"no row block for {rows} rows in {n_steps} steps")


def _attention(lq1, lk1, lq2, lk2, q, k, v, subln_g, norm_sq_max, f32_weights):
    seq = q.shape[0]
    tq = 1024
    tkc = 512
    n_q = seq // tq
    n_steps = N_DIFF_HEADS * n_q
    lam_spec = pl.BlockSpec((1, DIFF_HEAD_DIM), lambda h, i: (0, 0))
    cast_specs, cast_blocks = [], []
    for w in f32_weights:
        block = _cast_block_rows(w.shape[0], n_steps)
        n_blocks = w.shape[0] // block
        cast_blocks.append(n_blocks)
        cast_specs.append(pl.BlockSpec(
            (block, w.shape[1]), lambda h, i, last=n_blocks - 1: (jnp.minimum(h * n_q + i, last), 0)))
    outs = pl.pallas_call(
        functools.partial(_attn_kernel, tkc=tkc, row_block=64, unroll=16, cast_blocks=tuple(cast_blocks)),
        grid=(N_DIFF_HEADS, n_q),
        in_specs=[
            lam_spec, lam_spec, lam_spec, lam_spec,
            pl.BlockSpec((tq, DIFF_V_DIM), lambda h, i: (i, h)),
            pl.BlockSpec((seq, DIFF_V_DIM), lambda h, i: (0, h)),
            pl.BlockSpec((seq, DIFF_V_DIM), lambda h, i: (0, h)),
            pl.BlockSpec((1, DIFF_V_DIM), lambda h, i: (0, 0)),
            pl.BlockSpec(norm_sq_max.shape, lambda h, i: (0, 0, 0)),
        ] + cast_specs,
        out_specs=[pl.BlockSpec((tq, DIFF_V_DIM), lambda h, i: (i, h))] + cast_specs,
        out_shape=[jax.ShapeDtypeStruct((seq, N_DIFF_HEADS * DIFF_V_DIM), BF16)]
        + [jax.ShapeDtypeStruct(w.shape, BF16) for w in f32_weights],
        scratch_shapes=[pltpu.VMEM((2, seq // tkc, tq, tkc), BF16),
                        pltpu.VMEM((2, tq, LANES), F32),
                        pltpu.VMEM((2, tq, LANES), F32),
                        pltpu.VMEM((tq, LANES), F32)],
        compiler_params=_params(("arbitrary", "arbitrary")),
        name="diff_attention",
    )(lq1, lk1, lq2, lk2, q, k, v, subln_g, norm_sq_max, *f32_weights)
    return outs[0], outs[1:]


def _outproj_kernel(a_ref, gm_ref, w_ref, x_ref, gpost_ref, gffn_ref, x1_ref, h2_ref, *, row_block):
    tm, half = a_ref.shape
    for r in range(0, tm, row_block):
        rows = slice(r, r + row_block)
        mix = (jnp.dot(a_ref[rows, :], w_ref[0:half, :], preferred_element_type=F32)
               + jnp.dot(gm_ref[rows, :], w_ref[half:2 * half, :], preferred_element_type=F32))
        x1 = x_ref[rows, :] + _rms(mix, RMS_EPS) * gpost_ref[...]
        x1_ref[rows, :] = x1
        h2_ref[rows, :] = (_rms(x1, RMS_EPS) * gffn_ref[...]).astype(BF16)


def _outproj(attn, gm, w_out, x, post_mix_g, pre_ffn_g):
    seq, d_model = x.shape
    half = attn.shape[1]
    tm = 512
    row_spec = lambda cols: pl.BlockSpec((tm, cols), lambda i: (i, 0))
    const_spec = lambda shape: pl.BlockSpec(shape, lambda i: (0, 0))
    return pl.pallas_call(
        functools.partial(_outproj_kernel, row_block=128),
        grid=(seq // tm,),
        in_specs=[row_spec(half), row_spec(half), const_spec(w_out.shape), row_spec(d_model),
                  const_spec((1, d_model)), const_spec((1, d_model))],
        out_specs=[row_spec(d_model), row_spec(d_model)],
        out_shape=[jax.ShapeDtypeStruct((seq, d_model), F32), jax.ShapeDtypeStruct((seq, d_model), BF16)],
        compiler_params=_params(("arbitrary",)),
        name="outproj",
    )(attn, gm, w_out, x, post_mix_g, pre_ffn_g)


def _ffn_kernel(h_ref, wg_ref, wu_ref, wd_ref, x1_ref, g_ref, o_ref, acc_ref):
    j = pl.program_id(1)

    @pl.when(j == 0)
    def _():
        acc_ref[...] = jnp.zeros_like(acc_ref)

    h = h_ref[...]
    gate = jnp.dot(h, wg_ref[...], preferred_element_type=F32)
    up = jnp.dot(h, wu_ref[...], preferred_element_type=F32)
    act = (gate * jax.nn.sigmoid(gate) * up).astype(BF16)
    acc_ref[...] += jnp.dot(act, wd_ref[...], preferred_element_type=F32)

    @pl.when(j == pl.num_programs(1) - 1)
    def _():
        o_ref[...] = x1_ref[...] + _rms(acc_ref[...], RMS_EPS) * g_ref[...]


def _ffn(h2, w_gate, w_up, w_down, x1, post_ffn_g):
    seq, d_model = x1.shape
    d_ff = w_gate.shape[1]
    tm = 512
    tf = 512
    return pl.pallas_call(
        _ffn_kernel,
        grid=(seq // tm, d_ff // tf),
        in_specs=[
            pl.BlockSpec((tm, d_model), lambda i, j: (i, 0)),
            pl.BlockSpec((d_model, tf), lambda i, j: (0, j)),
            pl.BlockSpec((d_model, tf), lambda i, j: (0, j)),
            pl.BlockSpec((tf, d_model), lambda i, j: (j, 0)),
            pl.BlockSpec((tm, d_model), lambda i, j: (i, 0)),
            pl.BlockSpec((1, d_model), lambda i, j: (0, 0)),
        ],
        out_specs=pl.BlockSpec((tm, d_model), lambda i, j: (i, 0)),
        out_shape=jax.ShapeDtypeStruct((seq, d_model), F32),
        scratch_shapes=[pltpu.VMEM((tm, d_model), F32)],
        compiler_params=_params(("arbitrary", "arbitrary")),
        name="ffn",
    )(h2, w_gate, w_up, w_down, x1, post_ffn_g)


def kernel(x, positions, pre_mix_g, w_in, lambda_q1, lambda_k1, lambda_q2, lambda_k2, subln_g,
           gmlp_ln_g, gmlp_ln_b, w_s, b_s, w_out, post_mix_g, pre_ffn_g, w_gate, w_up, w_down,
           post_ffn_g):
    batch, seq, d_model = x.shape
    assert batch == 1 and pre_mix_g.shape[0] == 1, "single sequence, single layer"
    assert seq % 1024 == 0 and d_model == 2048
    x2 = x[0]
    inv_freq = ROPE_THETA ** (-jnp.arange(0, DIFF_HEAD_DIM, 2, dtype=F32) / DIFF_HEAD_DIM)
    inv_lanes = jnp.tile(inv_freq, LANES // (DIFF_HEAD_DIM // 2))[None, :]
    bs_full = jnp.repeat(b_s[0].T, LANES, axis=1)
    q, k, v, gm, norm_sq_max = _inproj(x2, pre_mix_g, w_in[0].astype(BF16), positions[0][:, None], inv_lanes,
                          gmlp_ln_g, gmlp_ln_b, w_s[0].astype(BF16), bs_full)
    attn, (w_out16, w_gate16, w_up16, w_down16) = _attention(
        lambda_q1, lambda_k1, lambda_q2, lambda_k2, q, k, v, subln_g, norm_sq_max,
        (w_out[0], w_gate[0], w_up[0], w_down[0]))
    x1, h2 = _outproj(attn, gm, w_out16, x2, post_mix_g, pre_ffn_g)
    out = _ffn(h2, w_gate16, w_up16, w_down16, x1, post_ffn_g)
    return out[None]
```

```python
import functools
import math

import jax
import jax.numpy as jnp
from jax import lax
from jax.experimental import pallas as pl
from jax.experimental.pallas import tpu as pltpu

N_DIFF_HEADS = 8
DIFF_HEAD_DIM = 64
DIFF_V_DIM = 2 * DIFF_HEAD_DIM
N_GMLP_GROUPS = 8
CHUNK = 128
ROPE_THETA = 10000.0
RMS_EPS = 1e-6
LN_EPS = 1e-5
SUBLN_EPS = 1e-5
LAMBDA_INIT = 0.8 - 0.6 * math.exp(-0.3 * 0)

LANES = 128
VMEM_LIMIT_BYTES = 56 * 1024 * 1024

F32 = jnp.float32
BF16 = jnp.bfloat16


def _rms(x, eps):
    return x * lax.rsqrt(jnp.mean(x * x, axis=-1, keepdims=True) + eps)


def _gelu(x):
    return 0.5 * x * (1.0 + lax.erf(x * (1.0 / math.sqrt(2.0))))


def _params(semantics):
    return pltpu.CompilerParams(dimension_semantics=semantics, vmem_limit_bytes=VMEM_LIMIT_BYTES)


def _first_half_lanes(shape):
    lane = lax.broadcasted_iota(jnp.int32, shape, 1)
    return (lane % DIFF_HEAD_DIM) < (DIFF_HEAD_DIM // 2)


def _rope(x, cos, sin_signed):
    half = DIFF_HEAD_DIM // 2
    partner = jnp.where(_first_half_lanes(x.shape), pltpu.roll(x, LANES - half, 1), pltpu.roll(x, half, 1))
    return x * cos + partner * sin_signed


def _inproj_kernel(x_ref, g_ref, w_ref, pos_ref, inv_ref, lng_ref, lnb_ref, ws_ref, bs_ref,
                   q_ref, k_ref, v_ref, gm_ref, nmax_ref, h_ref, gu_ref, *, q_scale):
    tm, width = q_ref.shape
    q_cols, k_cols, v_cols, gu_cols, gv_cols = (slice(n * width, (n + 1) * width) for n in range(5))

    def proj(cols):
        return jnp.dot(h_ref[...], w_ref[:, cols], preferred_element_type=F32)

    h_ref[...] = (_rms(x_ref[...], RMS_EPS) * g_ref[...]).astype(BF16)
    v_ref[...] = proj(v_cols).astype(BF16)

    ang = pos_ref[...].astype(F32) * inv_ref[...]
    sin = jnp.sin(ang)
    sin = jnp.where(_first_half_lanes(ang.shape), -sin, sin)
    cos = jnp.cos(ang)

    map1_lanes = lax.broadcasted_iota(jnp.int32, (tm, LANES), 1) < DIFF_HEAD_DIM
    nmax_lane = lax.broadcasted_iota(jnp.int32, nmax_ref.shape[1:], 1)

    def rope_store(out_ref, cols, scale, nmax, first_lane):
        acc = proj(cols)
        for c in range(width // LANES):
            sl = slice(c * LANES, (c + 1) * LANES)
            rotated = (_rope(acc[:, sl], cos, sin) * scale).astype(BF16)
            out_ref[:, sl] = rotated
            sq = rotated.astype(F32)
            sq = sq * sq
            for m, in_map in enumerate((map1_lanes, jnp.logical_not(map1_lanes))):
                norm_sq = jnp.sum(jnp.where(in_map, sq, 0.0), axis=1, keepdims=True)
                nmax = jnp.where(nmax_lane == first_lane + 2 * c + m,
                                 jnp.max(norm_sq, axis=0, keepdims=True), nmax)
        return nmax

    nmax = rope_store(q_ref, q_cols, q_scale, jnp.zeros(nmax_ref.shape[1:], F32), 0)
    nmax_ref[0] = rope_store(k_ref, k_cols, 1.0, nmax, 2 * N_DIFF_HEADS)

    gu_ref[...] = _gelu(proj(gu_cols))
    gv = _gelu(proj(gv_cols))
    mu = jnp.mean(gv, axis=-1, keepdims=True)
    d = gv - mu
    var = jnp.mean(d * d, axis=-1, keepdims=True)
    vln = (d * lax.rsqrt(var + LN_EPS) * lng_ref[...] + lnb_ref[...]).astype(BF16)
    for c in range(tm // CHUNK):
        rows = slice(c * CHUNK, (c + 1) * CHUNK)
        for g in range(N_GMLP_GROUPS):
            cols = slice(g * LANES, (g + 1) * LANES)
            y = jnp.dot(ws_ref[g], vln[rows, cols], preferred_element_type=F32) + bs_ref[:, cols]
            gm_ref[rows, cols] = (gu_ref[rows, cols] * y).astype(BF16)


def _inproj(x, pre_g, w_in, positions, inv_lanes, ln_g, ln_b, w_s, bs_full):
    seq, d_model = x.shape
    width = w_in.shape[1] // 5
    tm = 512
    row_spec = lambda cols: pl.BlockSpec((tm, cols), lambda i: (i, 0))
    const_spec = lambda shape, **kw: pl.BlockSpec(shape, lambda i: (0,) * len(shape), **kw)
    return pl.pallas_call(
        functools.partial(_inproj_kernel, q_scale=DIFF_HEAD_DIM ** -0.5 * math.log2(math.e)),
        grid=(seq // tm,),
        in_specs=[
            row_spec(d_model),
            const_spec((1, d_model)),
            const_spec(w_in.shape, pipeline_mode=pl.Buffered(1)),
            row_spec(1),
            const_spec((1, LANES)),
            const_spec((1, width)),
            const_spec((1, width)),
            const_spec(w_s.shape),
            const_spec(bs_full.shape),
        ],
        out_specs=[row_spec(width)] * 4 + [pl.BlockSpec((1, 8, LANES), lambda i: (i, 0, 0))],
        out_shape=[jax.ShapeDtypeStruct((seq, width), BF16)] * 4
        + [jax.ShapeDtypeStruct((seq // tm, 8, LANES), F32)],
        scratch_shapes=[pltpu.VMEM((tm, d_model), BF16), pltpu.VMEM((tm, width), F32)],
        compiler_params=_params(("arbitrary",)),
        name="inproj",
    )(x, pre_g, w_in, positions, inv_lanes, ln_g, ln_b, w_s, bs_full)


def _attn_kernel(*refs, tkc, row_block, unroll, cast_blocks):
    n_cast = len(cast_blocks)
    lq1_ref, lk1_ref, lq2_ref, lk2_ref, q_ref, k_ref, v_ref, g_ref, nmax_ref = refs[:9]
    cast_src = refs[9:9 + n_cast]
    o_ref = refs[9 + n_cast]
    cast_dst = refs[10 + n_cast:10 + 2 * n_cast]
    p_ref, shift_ref, lpart_ref, acc_ref = refs[10 + 2 * n_cast:]

    step = pl.program_id(0) * pl.num_programs(1) + pl.program_id(1)
    for src, dst, n_blocks in zip(cast_src, cast_dst, cast_blocks):
        @pl.when(step < n_blocks)
        def _(src=src, dst=dst):
            dst[...] = src[...].astype(BF16)

    tq = q_ref.shape[0]
    nkc = k_ref.shape[0] // tkc
    lane = lax.broadcasted_iota(jnp.int32, (tq, LANES), 1)
    in_map1 = lane < DIFF_HEAD_DIM
    q = q_ref[...]
    zero = jnp.zeros_like(q)
    q1 = jnp.where(in_map1, q, zero)
    q2 = jnp.where(in_map1, zero, q)
    nt_dims = (((1,), (1,)), ((), ()))

    def key_rows(kc):
        return pl.ds(pl.multiple_of(kc * tkc, tkc), tkc)

    row_blocks = [slice(r, r + row_block) for r in range(0, tq, row_block)]
    lane_tiles = [slice(t, t + LANES) for t in range(0, tkc, LANES)]

    def sweep(body, unroll):
        lax.fori_loop(0, nkc, lambda kc, c: (body(kc), c)[1], 0, unroll=unroll)

    norm_sq_max = jnp.max(nmax_ref[...], axis=0)
    nmax_lane = lax.broadcasted_iota(jnp.int32, norm_sq_max.shape, 1)
    head_lane = 2 * pl.program_id(0)

    def norm_sq(lane_index):
        picked = jnp.where(nmax_lane == lane_index, norm_sq_max, 0.0)
        return jnp.max(jnp.max(picked, axis=1, keepdims=True), axis=0, keepdims=True)

    score_bound = [jnp.sqrt(norm_sq(head_lane + mp) * norm_sq(2 * N_DIFF_HEADS + head_lane + mp))
                   for mp in range(2)]

    lam =(jnp.exp(jnp.sum(lq1_ref[...] * lk1_ref[...], axis=1, keepdims=True))
           - jnp.exp(jnp.sum(lq2_ref[...] * lk2_ref[...], axis=1, keepdims=True)) + LAMBDA_INIT)

    def lane_sums(x):
        hi = x.astype(BF16)
        lo = (x - hi.astype(F32)).astype(BF16)
        return jnp.dot(jnp.concatenate([hi, lo], axis=1), jnp.ones((2 * LANES, LANES), BF16),
                       preferred_element_type=F32)

    def emit(out):
        mean_sq = lane_sums(out * out) * (1.0 / LANES)
        scale = lax.rsqrt(mean_sq + SUBLN_EPS) * g_ref[...] * (1.0 - LAMBDA_INIT)
        o_ref[...] = (out * scale).astype(BF16)

    def row_sums():
        sums = lane_sums(lpart_ref[...].reshape(2 * tq, LANES))
        return sums[:tq], sums[tq:]

    q_maps = jnp.concatenate([q1, q2], axis=0)

    def map_scores(kc):
        return lax.dot_general(q_maps, k_ref[key_rows(kc), :], nt_dims, preferred_element_type=F32)

    def map_tiles(update):
        for mp in range(2):
            for rows in row_blocks:
                update(mp, rows, slice(mp * tq + rows.start, mp * tq + rows.stop))

    def shifted_probs(kc):
        s = map_scores(kc)

        def update(mp, rows, s_rows):
            shift = shift_ref[mp, rows]
            l = lpart_ref[mp, rows]
            for t in lane_tiles:
                p = jnp.exp2(s[s_rows, t] - shift)
                p_ref[mp, kc, rows, t] = p.astype(BF16)
                l = l + p
            lpart_ref[mp, rows] = l

        map_tiles(update)

    for mp in range(2):
        shift_ref[mp] = jnp.broadcast_to(score_bound[mp], (tq, LANES))
    lpart_ref[...] = jnp.zeros_like(lpart_ref)
    sweep(shifted_probs, unroll)
    bounded_sums = row_sums()

    def redo_with_row_max():
        lpart_ref[...] = jnp.full(lpart_ref.shape, -jnp.inf, F32)

        def row_max(kc):
            s = map_scores(kc)

            def update(mp, rows, s_rows):
                m = lpart_ref[mp, rows]
                for t in lane_tiles:
                    m = jnp.maximum(m, s[s_rows, t])
                lpart_ref[mp, rows] = m

            map_tiles(update)

        sweep(row_max, 1)
        for mp in range(2):
            shift_ref[mp] = jnp.broadcast_to(jnp.max(lpart_ref[mp], axis=1, keepdims=True), (tq, LANES))
        lpart_ref[...] = jnp.zeros_like(lpart_ref)
        sweep(shifted_probs, 1)
        return row_sums()

    sums_ok = jnp.minimum(jnp.min(bounded_sums[0]), jnp.min(bounded_sums[1])) >= 2.0 ** -60
    l1, l2 = lax.cond(sums_ok, lambda: bounded_sums, redo_with_row_max)

    ratio = (lam * l1 / l2).astype(BF16)
    acc_ref[...] = jnp.zeros_like(acc_ref)

    def combined_values(kc):
        w = jnp.concatenate([p_ref[0, kc, :, t] - p_ref[1, kc, :, t] * ratio for t in lane_tiles], axis=1)
        acc_ref[...] += jnp.dot(w, v_ref[key_rows(kc), :], preferred_element_type=F32)

    sweep(combined_values, unroll)
    emit(acc_ref[...] * (1.0 / l1))


BF16_SUBLANES = 16


def _cast_block_rows(rows, n_steps):
    for block in range(BF16_SUBLANES, rows + 1, BF16_SUBLANES):
        if rows % block == 0 and rows // block <= n_steps:
            return block
    raise ValueError(f"no row block for {rows} rows in {n_steps} steps")


def _attention(lq1, lk1, lq2, lk2, q, k, v, subln_g, norm_sq_max, f32_weights):
    seq = q.shape[0]
    tq = 1024
    tkc = 512
    n_q = seq // tq
    n_steps = N_DIFF_HEADS * n_q
    lam_spec = pl.BlockSpec((1, DIFF_HEAD_DIM), lambda h, i: (0, 0))
    cast_specs, cast_blocks = [], []
    for w in f32_weights:
        block = _cast_block_rows(w.shape[0], n_steps)
        n_blocks = w.shape[0] // block
        cast_blocks.append(n_blocks)
        cast_specs.append(pl.BlockSpec(
            (block, w.shape[1]), lambda h, i, last=n_blocks - 1: (jnp.minimum(h * n_q + i, last), 0)))
    outs = pl.pallas_call(
        functools.partial(_attn_kernel, tkc=tkc, row_block=64, unroll=16, cast_blocks=tuple(cast_blocks)),
        grid=(N_DIFF_HEADS, n_q),
        in_specs=[
            lam_spec, lam_spec, lam_spec, lam_spec,
            pl.BlockSpec((tq, DIFF_V_DIM), lambda h, i: (i, h)),
            pl.BlockSpec((seq, DIFF_V_DIM), lambda h, i: (0, h)),
            pl.BlockSpec((seq, DIFF_V_DIM), lambda h, i: (0, h)),
            pl.BlockSpec((1, DIFF_V_DIM), lambda h, i: (0, 0)),
            pl.BlockSpec(norm_sq_max.shape, lambda h, i: (0, 0, 0)),
        ] + cast_specs,
        out_specs=[pl.BlockSpec((tq, DIFF_V_DIM), lambda h, i: (i, h))] + cast_specs,
        out_shape=[jax.ShapeDtypeStruct((seq, N_DIFF_HEADS * DIFF_V_DIM), BF16)]
        + [jax.ShapeDtypeStruct(w.shape, BF16) for w in f32_weights],
        scratch_shapes=[pltpu.VMEM((2, seq // tkc, tq, tkc), BF16),
                        pltpu.VMEM((2, tq, LANES), F32),
                        pltpu.VMEM((2, tq, LANES), F32),
                        pltpu.VMEM((tq, LANES), F32)],
        compiler_params=_params(("arbitrary", "arbitrary")),
        name="diff_attention",
    )(lq1, lk1, lq2, lk2, q, k, v, subln_g, norm_sq_max, *f32_weights)
    return outs[0], outs[1:]


def _outproj_kernel(a_ref, gm_ref, w_ref, x_ref, gpost_ref, gffn_ref, x1_ref, h2_ref, *, row_block):
    tm, half = a_ref.shape
    for r in range(0, tm, row_block):
        rows = slice(r, r + row_block)
        mix = (jnp.dot(a_ref[rows, :], w_ref[0:half, :], preferred_element_type=F32)
               + jnp.dot(gm_ref[rows, :], w_ref[half:2 * half, :], preferred_element_type=F32))
        x1 = x_ref[rows, :] + _rms(mix, RMS_EPS) * gpost_ref[...]
        x1_ref[rows, :] = x1
        h2_ref[rows, :] = (_rms(x1, RMS_EPS) * gffn_ref[...]).astype(BF16)


def _outproj(attn, gm, w_out, x, post_mix_g, pre_ffn_g):
    seq, d_model = x.shape
    half = attn.shape[1]
    tm = 512
    row_spec = lambda cols: pl.BlockSpec((tm, cols), lambda i: (i, 0))
    const_spec = lambda shape: pl.BlockSpec(shape, lambda i: (0, 0))
    return pl.pallas_call(
        functools.partial(_outproj_kernel, row_block=128),
        grid=(seq // tm,),
        in_specs=[row_spec(half), row_spec(half), const_spec(w_out.shape), row_spec(d_model),
                  const_spec((1, d_model)), const_spec((1, d_model))],
        out_specs=[row_spec(d_model), row_spec(d_model)],
        out_shape=[jax.ShapeDtypeStruct((seq, d_model), F32), jax.ShapeDtypeStruct((seq, d_model), BF16)],
        compiler_params=_params(("arbitrary",)),
        name="outproj",
    )(attn, gm, w_out, x, post_mix_g, pre_ffn_g)


def _ffn_kernel(h_ref, wg_ref, wu_ref, wd_ref, x1_ref, g_ref, o_ref, acc_ref):
    j = pl.program_id(1)

    @pl.when(j == 0)
    def _():
        acc_ref[...] = jnp.zeros_like(acc_ref)

    h = h_ref[...]
    gate = jnp.dot(h, wg_ref[...], preferred_element_type=F32)
    up = jnp.dot(h, wu_ref[...], preferred_element_type=F32)
    act = (gate * jax.nn.sigmoid(gate) * up).astype(BF16)
    acc_ref[...] += jnp.dot(act, wd_ref[...], preferred_element_type=F32)

    @pl.when(j == pl.num_programs(1) - 1)
    def _():
        o_ref[...] = x1_ref[...] + _rms(acc_ref[...], RMS_EPS) * g_ref[...]


def _ffn(h2, w_gate, w_up, w_down, x1, post_ffn_g):
    seq, d_model = x1.shape
    d_ff = w_gate.shape[1]
    tm = 512
    tf = 512
    return pl.pallas_call(
        _ffn_kernel,
        grid=(seq // tm, d_ff // tf),
        in_specs=[
            pl.BlockSpec((tm, d_model), lambda i, j: (i, 0)),
            pl.BlockSpec((d_model, tf), lambda i, j: (0, j)),
            pl.BlockSpec((d_model, tf), lambda i, j: (0, j)),
            pl.BlockSpec((tf, d_model), lambda i, j: (j, 0)),
            pl.BlockSpec((tm, d_model), lambda i, j: (i, 0)),
            pl.BlockSpec((1, d_model), lambda i, j: (0, 0)),
        ],
        out_specs=pl.BlockSpec((tm, d_model), lambda i, j: (i, 0)),
        out_shape=jax.ShapeDtypeStruct((seq, d_model), F32),
        scratch_shapes=[pltpu.VMEM((tm, d_model), F32)],
        compiler_params=_params(("arbitrary", "arbitrary")),
        name="ffn",
    )(h2, w_gate, w_up, w_down, x1, post_ffn_g)


def kernel(x, positions, pre_mix_g, w_in, lambda_q1, lambda_k1, lambda_q2, lambda_k2, subln_g,
           gmlp_ln_g, gmlp_ln_b, w_s, b_s, w_out, post_mix_g, pre_ffn_g, w_gate, w_up, w_down,
           post_ffn_g):
    batch, seq, d_model = x.shape
    assert batch == 1 and pre_mix_g.shape[0] == 1, "single sequence, single layer"
    assert seq % 1024 == 0 and d_model == 2048
    x2 = x[0]
    inv_freq = ROPE_THETA ** (-jnp.arange(0, DIFF_HEAD_DIM, 2, dtype=F32) / DIFF_HEAD_DIM)
    inv_lanes = jnp.tile(inv_freq, LANES // (DIFF_HEAD_DIM // 2))[None, :]
    bs_full = jnp.repeat(b_s[0].T, LANES, axis=1)
    q, k, v, gm, norm_sq_max = _inproj(x2, pre_mix_g, w_in[0].astype(BF16), positions[0][:, None], inv_lanes,
                          gmlp_ln_g, gmlp_ln_b, w_s[0].astype(BF16), bs_full)
    attn, (w_out16, w_gate16, w_up16, w_down16) = _attention(
        lambda_q1, lambda_k1, lambda_q2, lambda_k2, q, k, v, subln_g, norm_sq_max,
        (w_out[0], w_gate[0], w_up[0], w_down[0]))
    x1, h2 = _outproj(attn, gm, w_out16, x2, post_mix_g, pre_ffn_g)
    out = _ffn(h2, w_gate16, w_up16, w_down16, x1, post_ffn_g)
    return out[None]
```

```python
import functools
import math

import jax
import jax.numpy as jnp
from jax import lax
from jax.experimental import pallas as pl
from jax.experimental.pallas import tpu as pltpu

N_DIFF_HEADS = 8
DIFF_HEAD_DIM = 64
DIFF_V_DIM = 2 * DIFF_HEAD_DIM
N_GMLP_GROUPS = 8
CHUNK = 128
ROPE_THETA = 10000.0
RMS_EPS = 1e-6
LN_EPS = 1e-5
SUBLN_EPS = 1e-5
LAMBDA_INIT = 0.8 - 0.6 * math.exp(-0.3 * 0)

LANES = 128
VMEM_LIMIT_BYTES = 56 * 1024 * 1024

F32 = jnp.float32
BF16 = jnp.bfloat16


def _rms(x, eps):
    return x * lax.rsqrt(jnp.mean(x * x, axis=-1, keepdims=True) + eps)


def _gelu(x):
    return 0.5 * x * (1.0 + lax.erf(x * (1.0 / math.sqrt(2.0))))


def _params(semantics):
    return pltpu.CompilerParams(dimension_semantics=semantics, vmem_limit_bytes=VMEM_LIMIT_BYTES)


def _first_half_lanes(shape):
    lane = lax.broadcasted_iota(jnp.int32, shape, 1)
    return (lane % DIFF_HEAD_DIM) < (DIFF_HEAD_DIM // 2)


def _rope(x, cos, sin_signed):
    half = DIFF_HEAD_DIM // 2
    partner = jnp.where(_first_half_lanes(x.shape), pltpu.roll(x, LANES - half, 1), pltpu.roll(x, half, 1))
    return x * cos + partner * sin_signed


def _inproj_kernel(x_ref, g_ref, w_ref, pos_ref, inv_ref, lng_ref, lnb_ref, ws_ref, bs_ref,
                   q_ref, k_ref, vt_ref, gm_ref, nmax_ref, h_ref, gu_ref, *, q_scale):
    tm, width = q_ref.shape
    q_cols, k_cols, v_cols, gu_cols, gv_cols = (slice(n * width, (n + 1) * width) for n in range(5))

    def proj(cols):
        return jnp.dot(h_ref[...], w_ref[:, cols], preferred_element_type=F32)

    h_ref[...] = (_rms(x_ref[...], RMS_EPS) * g_ref[...]).astype(BF16)
    vt_ref[...] = proj(v_cols).T.astype(BF16)

    ang = pos_ref[...].astype(F32) * inv_ref[...]
    sin = jnp.sin(ang)
    sin = jnp.where(_first_half_lanes(ang.shape), -sin, sin)
    cos = jnp.cos(ang)

    map1_lanes = lax.broadcasted_iota(jnp.int32, (tm, LANES), 1) < DIFF_HEAD_DIM
    nmax_lane = lax.broadcasted_iota(jnp.int32, nmax_ref.shape[1:], 1)

    def rope_store(out_ref, cols, scale, nmax, first_lane):
        acc = proj(cols)
        for c in range(width // LANES):
            sl = slice(c * LANES, (c + 1) * LANES)
            rotated = (_rope(acc[:, sl], cos, sin) * scale).astype(BF16)
            out_ref[:, sl] = rotated
            sq = rotated.astype(F32)
            sq = sq * sq
            for m, in_map in enumerate((map1_lanes, jnp.logical_not(map1_lanes))):
                norm_sq = jnp.sum(jnp.where(in_map, sq, 0.0), axis=1, keepdims=True)
                nmax = jnp.where(nmax_lane == first_lane + 2 * c + m,
                                 jnp.max(norm_sq, axis=0, keepdims=True), nmax)
        return nmax

    nmax = rope_store(q_ref, q_cols, q_scale, jnp.zeros(nmax_ref.shape[1:], F32), 0)
    nmax_ref[0] = rope_store(k_ref, k_cols, 1.0, nmax, 2 * N_DIFF_HEADS)

    gu_ref[...] = _gelu(proj(gu_cols))
    gv = _gelu(proj(gv_cols))
    mu = jnp.mean(gv, axis=-1, keepdims=True)
    d = gv - mu
    var = jnp.mean(d * d, axis=-1, keepdims=True)
    vln = (d * lax.rsqrt(var + LN_EPS) * lng_ref[...] + lnb_ref[...]).astype(BF16)
    for c in range(tm // CHUNK):
        rows = slice(c * CHUNK, (c + 1) * CHUNK)
        for g in range(N_GMLP_GROUPS):
            cols = slice(g * LANES, (g + 1) * LANES)
            y = jnp.dot(ws_ref[g], vln[rows, cols], preferred_element_type=F32) + bs_ref[:, cols]
            gm_ref[rows, cols] = (gu_ref[rows, cols] * y).astype(BF16)


def _inproj(x, pre_g, w_in, positions, inv_lanes, ln_g, ln_b, w_s, bs_full):
    seq, d_model = x.shape
    width = w_in.shape[1] // 5
    tm = 512
    row_spec = lambda cols: pl.BlockSpec((tm, cols), lambda i: (i, 0))
    const_spec = lambda shape, **kw: pl.BlockSpec(shape, lambda i: (0,) * len(shape), **kw)
    return pl.pallas_call(
        functools.partial(_inproj_kernel, q_scale=DIFF_HEAD_DIM ** -0.5 * math.log2(math.e)),
        grid=(seq // tm,),
        in_specs=[
            row_spec(d_model),
            const_spec((1, d_model)),
            const_spec(w_in.shape, pipeline_mode=pl.Buffered(1)),
            row_spec(1),
            const_spec((1, LANES)),
            const_spec((1, width)),
            const_spec((1, width)),
            const_spec(w_s.shape),
            const_spec(bs_full.shape),
        ],
        out_specs=[row_spec(width), row_spec(width), pl.BlockSpec((width, tm), lambda i: (0, i)),
                   row_spec(width), pl.BlockSpec((1, 8, LANES), lambda i: (i, 0, 0))],
        out_shape=[jax.ShapeDtypeStruct((seq, width), BF16), jax.ShapeDtypeStruct((seq, width), BF16),
                   jax.ShapeDtypeStruct((width, seq), BF16), jax.ShapeDtypeStruct((seq, width), BF16),
                   jax.ShapeDtypeStruct((seq // tm, 8, LANES), F32)],
        scratch_shapes=[pltpu.VMEM((tm, d_model), BF16), pltpu.VMEM((tm, width), F32)],
        compiler_params=_params(("arbitrary",)),
        name="inproj",
    )(x, pre_g, w_in, positions, inv_lanes, ln_g, ln_b, w_s, bs_full)


def _attn_kernel(*refs, tkc, cast_blocks):
    n_cast = len(cast_blocks)
    lq1_ref, lk1_ref, lq2_ref, lk2_ref, q_ref, k_ref, vt_ref, g_ref, nmax_ref = refs[:9]
    cast_src = refs[9:9 + n_cast]
    o_ref = refs[9 + n_cast]
    cast_dst = refs[10 + n_cast:10 + 2 * n_cast]
    p_ref, shift_ref, lpart_ref, acc_ref = refs[10 + 2 * n_cast:]

    step = pl.program_id(0) * pl.num_programs(1) + pl.program_id(1)
    for src, dst, n_blocks in zip(cast_src, cast_dst, cast_blocks):
        @pl.when(step < n_blocks)
        def _(src=src, dst=dst):
            dst[...] = src[...].astype(BF16)

    tq = q_ref.shape[0]
    two_tq = 2 * tq
    nkc = k_ref.shape[0] // tkc
    lane = lax.broadcasted_iota(jnp.int32, (tq, LANES), 1)
    in_map1 = lane < DIFF_HEAD_DIM
    q = q_ref[...]
    zero = jnp.zeros_like(q)
    q_maps = jnp.concatenate([jnp.where(in_map1, q, zero), jnp.where(in_map1, zero, q)], axis=0)
    nt_dims = (((1,), (1,)), ((), ()))
    col_tiles = [slice(c, c + LANES) for c in range(0, two_tq, LANES)]

    def key_rows(kc):
        if isinstance(kc, int):
            return slice(kc * tkc, (kc + 1) * tkc)
        return pl.ds(pl.multiple_of(kc * tkc, tkc), tkc)

    def score_tiles(kc):
        s = lax.dot_general(k_ref[key_rows(kc), :], q_maps, nt_dims, preferred_element_type=F32)
        return [s[:, cols].reshape(tkc // 8, 8, LANES) for cols in col_tiles]

    def shifted_probs(kc):
        for cols, tile in zip(col_tiles, score_tiles(kc)):
            p = jnp.exp2(tile - shift_ref[:, cols])
            p_ref[kc, :, cols] = p.reshape(tkc, LANES).astype(BF16)
            lpart_ref[:, cols] += jnp.sum(p, axis=0)

    def column_max(kc):
        for cols, tile in zip(col_tiles, score_tiles(kc)):
            lpart_ref[:, cols] = jnp.maximum(lpart_ref[:, cols], jnp.max(tile, axis=0))

    def sublane_reduce(op):
        return op(lpart_ref[...], axis=0, keepdims=True)

    norm_sq_max = jnp.max(nmax_ref[...], axis=0)
    nmax_lane = lax.broadcasted_iota(jnp.int32, norm_sq_max.shape, 1)
    head_lane = 2 * pl.program_id(0)

    def norm_sq(lane_index):
        picked = jnp.where(nmax_lane == lane_index, norm_sq_max, 0.0)
        return jnp.max(jnp.max(picked, axis=1, keepdims=True), axis=0, keepdims=True)

    for mp in range(2):
        bound = jnp.sqrt(norm_sq(head_lane + mp) * norm_sq(2 * N_DIFF_HEADS + head_lane + mp))
        shift_ref[:, mp * tq:(mp + 1) * tq] = jnp.broadcast_to(bound, (8, tq))
    lpart_ref[...] = jnp.zeros_like(lpart_ref)
    for kc in range(nkc):
        shifted_probs(kc)
    bounded_sums = sublane_reduce(jnp.sum)

    def redo_with_column_max():
        lpart_ref[...] = jnp.full(lpart_ref.shape, -jnp.inf, F32)
        lax.fori_loop(0, nkc, lambda kc, c: (column_max(kc), c)[1], 0)
        shift_ref[...] = jnp.broadcast_to(sublane_reduce(jnp.max), shift_ref.shape)
        lpart_ref[...] = jnp.zeros_like(lpart_ref)
        lax.fori_loop(0, nkc, lambda kc, c: (shifted_probs(kc), c)[1], 0)
        return sublane_reduce(jnp.sum)

    sums = lax.cond(jnp.min(bounded_sums) >= 2.0 ** -60, lambda: bounded_sums, redo_with_column_max)
    l1 = sums[:, :tq]
    l2 = sums[:, tq:]

    lam = (jnp.exp(jnp.sum(lq1_ref[...] * lk1_ref[...], axis=1, keepdims=True))
           - jnp.exp(jnp.sum(lq2_ref[...] * lk2_ref[...], axis=1, keepdims=True)) + LAMBDA_INIT)
    ratio = jnp.broadcast_to(lam * l1 / l2, (BF16_SUBLANES, tq)).astype(BF16)
    acc_ref[...] = jnp.zeros_like(acc_ref)
    for kc in range(nkc):
        p1 = p_ref[kc, :, 0:tq].reshape(tkc // BF16_SUBLANES, BF16_SUBLANES, tq)
        p2 = p_ref[kc, :, tq:two_tq].reshape(tkc // BF16_SUBLANES, BF16_SUBLANES, tq)
        w = (p1 - p2 * ratio).reshape(tkc, tq)
        acc_ref[...] += jnp.dot(vt_ref[:, kc * tkc:(kc + 1) * tkc], w, preferred_element_type=F32)

    out_t = acc_ref[...] * (1.0 / l1)
    out_t = out_t * lax.rsqrt(jnp.mean(out_t * out_t, axis=0, keepdims=True) + SUBLN_EPS)
    o_ref[...] = (out_t.T * g_ref[...] * (1.0 - LAMBDA_INIT)).astype(BF16)


BF16_SUBLANES = 16


def _cast_block_rows(rows, n_steps):
    for block in range(BF16_SUBLANES, rows + 1, BF16_SUBLANES):
        if rows % block == 0 and rows // block <= n_steps:
            return block
    raise ValueError(f"no row block for {rows} rows in {n_steps} steps")


def _attention(lq1, lk1, lq2, lk2, q, k, v_t, subln_g, norm_sq_max, f32_weights):
    seq = q.shape[0]
    tq = 1024
    tkc = 512
    n_q = seq // tq
    n_steps = N_DIFF_HEADS * n_q
    lam_spec = pl.BlockSpec((1, DIFF_HEAD_DIM), lambda h, i: (0, 0))
    cast_specs, cast_blocks = [], []
    for w in f32_weights:
        block = _cast_block_rows(w.shape[0], n_steps)
        n_blocks = w.shape[0] // block
        cast_blocks.append(n_blocks)
        cast_specs.append(pl.BlockSpec(
            (block, w.shape[1]), lambda h, i, last=n_blocks - 1: (jnp.minimum(h * n_q + i, last), 0)))
    outs = pl.pallas_call(
        functools.partial(_attn_kernel, tkc=tkc, cast_blocks=tuple(cast_blocks)),
        grid=(N_DIFF_HEADS, n_q),
        in_specs=[
            lam_spec, lam_spec, lam_spec, lam_spec,
            pl.BlockSpec((tq, DIFF_V_DIM), lambda h, i: (i, h)),
            pl.BlockSpec((seq, DIFF_V_DIM), lambda h, i: (0, h)),
            pl.BlockSpec((DIFF_V_DIM, seq), lambda h, i: (h, 0)),
            pl.BlockSpec((1, DIFF_V_DIM), lambda h, i: (0, 0)),
            pl.BlockSpec(norm_sq_max.shape, lambda h, i: (0, 0, 0)),
        ] + cast_specs,
        out_specs=[pl.BlockSpec((tq, DIFF_V_DIM), lambda h, i: (i, h))] + cast_specs,
        out_shape=[jax.ShapeDtypeStruct((seq, N_DIFF_HEADS * DIFF_V_DIM), BF16)]
        + [jax.ShapeDtypeStruct(w.shape, BF16) for w in f32_weights],
        scratch_shapes=[pltpu.VMEM((seq // tkc, tkc, 2 * tq), BF16),
                        pltpu.VMEM((8, 2 * tq), F32),
                        pltpu.VMEM((8, 2 * tq), F32),
                        pltpu.VMEM((DIFF_V_DIM, tq), F32)],
        compiler_params=_params(("arbitrary", "arbitrary")),
        name="diff_attention",
    )(lq1, lk1, lq2, lk2, q, k, v_t, subln_g, norm_sq_max, *f32_weights)
    return outs[0], outs[1:]


def _outproj_kernel(a_ref, gm_ref, w_ref, x_ref, gpost_ref, gffn_ref, x1_ref, h2_ref, *, row_block):
    tm, half = a_ref.shape
    for r in range(0, tm, row_block):
        rows = slice(r, r + row_block)
        mix = (jnp.dot(a_ref[rows, :], w_ref[0:half, :], preferred_element_type=F32)
               + jnp.dot(gm_ref[rows, :], w_ref[half:2 * half, :], preferred_element_type=F32))
        x1 = x_ref[rows, :] + _rms(mix, RMS_EPS) * gpost_ref[...]
        x1_ref[rows, :] = x1
        h2_ref[rows, :] = (_rms(x1, RMS_EPS) * gffn_ref[...]).astype(BF16)


def _outproj(attn, gm, w_out, x, post_mix_g, pre_ffn_g):
    seq, d_model = x.shape
    half = attn.shape[1]
    tm = 512
    row_spec = lambda cols: pl.BlockSpec((tm, cols), lambda i: (i, 0))
    const_spec = lambda shape: pl.BlockSpec(shape, lambda i: (0, 0))
    return pl.pallas_call(
        functools.partial(_outproj_kernel, row_block=128),
        grid=(seq // tm,),
        in_specs=[row_spec(half), row_spec(half), const_spec(w_out.shape), row_spec(d_model),
                  const_spec((1, d_model)), const_spec((1, d_model))],
        out_specs=[row_spec(d_model), row_spec(d_model)],
        out_shape=[jax.ShapeDtypeStruct((seq, d_model), F32), jax.ShapeDtypeStruct((seq, d_model), BF16)],
        compiler_params=_params(("arbitrary",)),
        name="outproj",
    )(attn, gm, w_out, x, post_mix_g, pre_ffn_g)


def _ffn_kernel(h_ref, wg_ref, wu_ref, wd_ref, x1_ref, g_ref, o_ref, acc_ref):
    j = pl.program_id(1)

    @pl.when(j == 0)
    def _():
        acc_ref[...] = jnp.zeros_like(acc_ref)

    h = h_ref[...]
    gate = jnp.dot(h, wg_ref[...], preferred_element_type=F32)
    up = jnp.dot(h, wu_ref[...], preferred_element_type=F32)
    act = (gate * jax.nn.sigmoid(gate) * up).astype(BF16)
    acc_ref[...] += jnp.dot(act, wd_ref[...], preferred_element_type=F32)

    @pl.when(j == pl.num_programs(1) - 1)
    def _():
        o_ref[...] = x1_ref[...] + _rms(acc_ref[...], RMS_EPS) * g_ref[...]


def _ffn(h2, w_gate, w_up, w_down, x1, post_ffn_g):
    seq, d_model = x1.shape
    d_ff = w_gate.shape[1]
    tm = 512
    tf = 512
    return pl.pallas_call(
        _ffn_kernel,
        grid=(seq // tm, d_ff // tf),
        in_specs=[
            pl.BlockSpec((tm, d_model), lambda i, j: (i, 0)),
            pl.BlockSpec((d_model, tf), lambda i, j: (0, j)),
            pl.BlockSpec((d_model, tf), lambda i, j: (0, j)),
            pl.BlockSpec((tf, d_model), lambda i, j: (j, 0)),
            pl.BlockSpec((tm, d_model), lambda i, j: (i, 0)),
            pl.BlockSpec((1, d_model), lambda i, j: (0, 0)),
        ],
        out_specs=pl.BlockSpec((tm, d_model), lambda i, j: (i, 0)),
        out_shape=jax.ShapeDtypeStruct((seq, d_model), F32),
        scratch_shapes=[pltpu.VMEM((tm, d_model), F32)],
        compiler_params=_params(("arbitrary", "arbitrary")),
        name="ffn",
    )(h2, w_gate, w_up, w_down, x1, post_ffn_g)


def kernel(x, positions, pre_mix_g, w_in, lambda_q1, lambda_k1, lambda_q2, lambda_k2, subln_g,
           gmlp_ln_g, gmlp_ln_b, w_s, b_s, w_out, post_mix_g, pre_ffn_g, w_gate, w_up, w_down,
           post_ffn_g):
    batch, seq, d_model = x.shape
    assert batch == 1 and pre_mix_g.shape[0] == 1, "single sequence, single layer"
    assert seq % 1024 == 0 and d_model == 2048
    x2 = x[0]
    inv_freq = ROPE_THETA ** (-jnp.arange(0, DIFF_HEAD_DIM, 2, dtype=F32) / DIFF_HEAD_DIM)
    inv_lanes = jnp.tile(inv_freq, LANES // (DIFF_HEAD_DIM // 2))[None, :]
    bs_full = jnp.repeat(b_s[0].T, LANES, axis=1)
    q, k, v_t, gm, norm_sq_max = _inproj(x2, pre_mix_g, w_in[0].astype(BF16), positions[0][:, None], inv_lanes,
                          gmlp_ln_g, gmlp_ln_b, w_s[0].astype(BF16), bs_full)
    attn, (w_out16, w_gate16, w_up16, w_down16) = _attention(
        lambda_q1, lambda_k1, lambda_q2, lambda_k2, q, k, v_t, subln_g, norm_sq_max,
        (w_out[0], w_gate[0], w_up[0], w_down[0]))
    x1, h2 = _outproj(attn, gm, w_out16, x2, post_mix_g, pre_ffn_g)
    out = _ffn(h2, w_gate16, w_up16, w_down16, x1, post_ffn_g)
    return out[None]
```

```python
import functools
import math

import jax
import jax.numpy as jnp
from jax import lax
from jax.experimental import pallas as pl
from jax.experimental.pallas import tpu as pltpu

N_DIFF_HEADS = 8
DIFF_HEAD_DIM = 64
DIFF_V_DIM = 2 * DIFF_HEAD_DIM
N_GMLP_GROUPS = 8
CHUNK = 128
ROPE_THETA = 10000.0
RMS_EPS = 1e-6
LN_EPS = 1e-5
SUBLN_EPS = 1e-5
LAMBDA_INIT = 0.8 - 0.6 * math.exp(-0.3 * 0)

LANES = 128
VMEM_LIMIT_BYTES = 60 * 1024 * 1024

F32 = jnp.float32
BF16 = jnp.bfloat16


def _rms(x, eps):
    return x * lax.rsqrt(jnp.mean(x * x, axis=-1, keepdims=True) + eps)


def _gelu(x):
    return 0.5 * x * (1.0 + lax.erf(x * (1.0 / math.sqrt(2.0))))


def _params(semantics):
    return pltpu.CompilerParams(dimension_semantics=semantics, vmem_limit_bytes=VMEM_LIMIT_BYTES)


def _first_half_lanes(shape):
    lane = lax.broadcasted_iota(jnp.int32, shape, 1)
    return (lane % DIFF_HEAD_DIM) < (DIFF_HEAD_DIM // 2)


def _rope(x, cos, sin_signed):
    half = DIFF_HEAD_DIM // 2
    partner = jnp.where(_first_half_lanes(x.shape), pltpu.roll(x, LANES - half, 1), pltpu.roll(x, half, 1))
    return x * cos + partner * sin_signed


def _inproj_kernel(x_ref, g_ref, w_ref, pos_ref, inv_ref, lng_ref, lnb_ref, ws_ref, bs_ref,
                   q_ref, k_ref, vt_ref, gm_ref, nmax_ref, h_ref, gu_ref, *, q_scale):
    tm, width = q_ref.shape
    q_cols, k_cols, v_cols, gu_cols, gv_cols = (slice(n * width, (n + 1) * width) for n in range(5))
    def proj(cols):
        return jnp.dot(h_ref[...], w_ref[:, cols], preferred_element_type=F32)

    h_ref[...] = (_rms(x_ref[...], RMS_EPS) * g_ref[...]).astype(BF16)
    vt_ref[...] = proj(v_cols).T.astype(BF16)

    ang = pos_ref[...].astype(F32) * inv_ref[...]
    sin = jnp.sin(ang)
    sin = jnp.where(_first_half_lanes(ang.shape), -sin, sin)
    cos = jnp.cos(ang)

    map1_lanes = lax.broadcasted_iota(jnp.int32, (tm, LANES), 1) < DIFF_HEAD_DIM
    nmax_lane = lax.broadcasted_iota(jnp.int32, nmax_ref.shape[1:], 1)

    def rope_store(out_ref, cols, scale, nmax, first_lane):
        acc = proj(cols)
        for c in range(width // LANES):
            sl = slice(c * LANES, (c + 1) * LANES)
            rotated = (_rope(acc[:, sl], cos, sin) * scale).astype(BF16)
            out_ref[:, sl] = rotated
            sq = rotated.astype(F32)
            sq = sq * sq
            for m, in_map in enumerate((map1_lanes, jnp.logical_not(map1_lanes))):
                norm_sq = jnp.sum(jnp.where(in_map, sq, 0.0), axis=1, keepdims=True)
                nmax = jnp.where(nmax_lane == first_lane + 2 * c + m,
                                 jnp.max(norm_sq, axis=0, keepdims=True), nmax)
        return nmax

    nmax = rope_store(q_ref, q_cols, q_scale, jnp.zeros(nmax_ref.shape[1:], F32), 0)
    nmax_ref[0] = rope_store(k_ref, k_cols, 1.0, nmax, 2 * N_DIFF_HEADS)

    gu_ref[...] = _gelu(proj(gu_cols))
    gv = _gelu(proj(gv_cols))
    mu = jnp.mean(gv, axis=-1, keepdims=True)
    d = gv - mu
    var = jnp.mean(d * d, axis=-1, keepdims=True)
    vln = (d * lax.rsqrt(var + LN_EPS) * lng_ref[...] + lnb_ref[...]).astype(BF16)
    for c in range(tm // CHUNK):
        rows = slice(c * CHUNK, (c + 1) * CHUNK)
        for g in range(N_GMLP_GROUPS):
            cols = slice(g * LANES, (g + 1) * LANES)
            y = jnp.dot(ws_ref[g], vln[rows, cols], preferred_element_type=F32) + bs_ref[:, cols]
            gm_ref[rows, cols] = (gu_ref[rows, cols] * y).astype(BF16)


def _inproj(x, pre_g, w_in, positions, inv_lanes, ln_g, ln_b, w_s, bs_full):
    seq, d_model = x.shape
    width = w_in.shape[1] // 5
    tm = 512
    row_spec = lambda cols: pl.BlockSpec((tm, cols), lambda i: (i, 0))
    const_spec = lambda shape, **kw: pl.BlockSpec(shape, lambda i: (0,) * len(shape), **kw)
    return pl.pallas_call(
        functools.partial(_inproj_kernel, q_scale=DIFF_HEAD_DIM ** -0.5 * math.log2(math.e)),
        grid=(seq // tm,),
        in_specs=[
            row_spec(d_model),
            const_spec((1, d_model)),
            const_spec(w_in.shape, pipeline_mode=pl.Buffered(1)),
            row_spec(1),
            const_spec((1, LANES)),
            const_spec((1, width)),
            const_spec((1, width)),
            const_spec(w_s.shape),
            const_spec(bs_full.shape),
        ],
        out_specs=[row_spec(width), row_spec(width), pl.BlockSpec((width, tm), lambda i: (0, i)),
                   row_spec(width), pl.BlockSpec((1, 8, LANES), lambda i: (i, 0, 0))],
        out_shape=[jax.ShapeDtypeStruct((seq, width), BF16), jax.ShapeDtypeStruct((seq, width), BF16),
                   jax.ShapeDtypeStruct((width, seq), BF16), jax.ShapeDtypeStruct((seq, width), BF16),
                   jax.ShapeDtypeStruct((seq // tm, 8, LANES), F32)],
        scratch_shapes=[pltpu.VMEM((tm, d_model), BF16), pltpu.VMEM((tm, width), F32)],
        compiler_params=_params(("arbitrary",)),
        name="inproj",
    )(x, pre_g, w_in, positions, inv_lanes, ln_g, ln_b, w_s, bs_full)


def _attn_kernel(*refs, tkc, cast_blocks):
    n_cast = len(cast_blocks)
    lq1_ref, lk1_ref, lq2_ref, lk2_ref, q_ref, k_ref, vt_ref, g_ref, nmax_ref = refs[:9]
    cast_src = refs[9:9 + n_cast]
    o_ref = refs[9 + n_cast]
    cast_dst = refs[10 + n_cast:10 + 2 * n_cast]
    p_ref, shift_ref, lpart_ref, acc_ref = refs[10 + 2 * n_cast:]

    step = pl.program_id(0) * pl.num_programs(1) + pl.program_id(1)
    for src, dst, n_blocks in zip(cast_src, cast_dst, cast_blocks):
        @pl.when(step < n_blocks)
        def _(src=src, dst=dst):
            dst[...] = src[...].astype(BF16)

    tq = q_ref.shape[0]
    two_tq = 2 * tq
    nkc = k_ref.shape[0] // tkc
    lane = lax.broadcasted_iota(jnp.int32, (tq, LANES), 1)
    in_map1 = lane < DIFF_HEAD_DIM
    q = q_ref[...]
    zero = jnp.zeros_like(q)
    q_maps = jnp.concatenate([jnp.where(in_map1, q, zero), jnp.where(in_map1, zero, q)], axis=0)
    nt_dims = (((1,), (1,)), ((), ()))
    col_tiles = [slice(c, c + LANES) for c in range(0, two_tq, LANES)]

    def key_rows(kc):
        if isinstance(kc, int):
            return slice(kc * tkc, (kc + 1) * tkc)
        return pl.ds(pl.multiple_of(kc * tkc, tkc), tkc)

    def score_tiles(kc):
        s = lax.dot_general(k_ref[key_rows(kc), :], q_maps, nt_dims, preferred_element_type=F32)
        return [s[:, cols].reshape(tkc // 8, 8, LANES) for cols in col_tiles]

    def shifted_probs(kc):
        for cols, tile in zip(col_tiles, score_tiles(kc)):
            p = jnp.exp2(tile - shift_ref[:, cols])
            p_ref[kc, :, cols] = p.reshape(tkc, LANES).astype(BF16)
            lpart_ref[:, cols] += jnp.sum(p, axis=0)

    def column_max(kc):
        for cols, tile in zip(col_tiles, score_tiles(kc)):
            lpart_ref[:, cols] = jnp.maximum(lpart_ref[:, cols], jnp.max(tile, axis=0))

    def sublane_reduce(op):
        return op(lpart_ref[...], axis=0, keepdims=True)

    norm_sq_max = jnp.max(nmax_ref[...], axis=0)
    nmax_lane = lax.broadcasted_iota(jnp.int32, norm_sq_max.shape, 1)
    head_lane = 2 * pl.program_id(0)

    def norm_sq(lane_index):
        picked = jnp.where(nmax_lane == lane_index, norm_sq_max, 0.0)
        return jnp.max(jnp.max(picked, axis=1, keepdims=True), axis=0, keepdims=True)

    for mp in range(2):
        bound = jnp.sqrt(norm_sq(head_lane + mp) * norm_sq(2 * N_DIFF_HEADS + head_lane + mp))
        shift_ref[:, mp * tq:(mp + 1) * tq] = jnp.broadcast_to(bound, (8, tq))
    lpart_ref[...] = jnp.zeros_like(lpart_ref)
    for kc in range(nkc):
        shifted_probs(kc)
    bounded_sums = sublane_reduce(jnp.sum)

    def redo_with_column_max():
        lpart_ref[...] = jnp.full(lpart_ref.shape, -jnp.inf, F32)
        lax.fori_loop(0, nkc, lambda kc, c: (column_max(kc), c)[1], 0)
        shift_ref[...] = jnp.broadcast_to(sublane_reduce(jnp.max), shift_ref.shape)
        lpart_ref[...] = jnp.zeros_like(lpart_ref)
        lax.fori_loop(0, nkc, lambda kc, c: (shifted_probs(kc), c)[1], 0)
        return sublane_reduce(jnp.sum)

    sums = lax.cond(jnp.min(bounded_sums) >= 2.0 ** -60, lambda: bounded_sums, redo_with_column_max)
    l1 = sums[:, :tq]
    l2 = sums[:, tq:]

    lam = (jnp.exp(jnp.sum(lq1_ref[...] * lk1_ref[...], axis=1, keepdims=True))
           - jnp.exp(jnp.sum(lq2_ref[...] * lk2_ref[...], axis=1, keepdims=True)) + LAMBDA_INIT)
    ratio = jnp.broadcast_to(lam * l1 / l2, (BF16_SUBLANES, tq)).astype(BF16)
    acc_ref[...] = jnp.zeros_like(acc_ref)
    for kc in range(nkc):
        p1 = p_ref[kc, :, 0:tq].reshape(tkc // BF16_SUBLANES, BF16_SUBLANES, tq)
        p2 = p_ref[kc, :, tq:two_tq].reshape(tkc // BF16_SUBLANES, BF16_SUBLANES, tq)
        w = (p1 - p2 * ratio).reshape(tkc, tq)
        acc_ref[...] += jnp.dot(vt_ref[:, kc * tkc:(kc + 1) * tkc], w, preferred_element_type=F32)

    out_t = acc_ref[...] * (1.0 / l1)
    out_t = out_t * lax.rsqrt(jnp.mean(out_t * out_t, axis=0, keepdims=True) + SUBLN_EPS)
    o_ref[...] = (out_t.T * g_ref[...] * (1.0 - LAMBDA_INIT)).astype(BF16)


BF16_SUBLANES = 16


def _cast_block_rows(rows, n_steps):
    for block in range(BF16_SUBLANES, rows + 1, BF16_SUBLANES):
        if rows % block == 0 and rows // block <= n_steps:
            return block
    raise ValueError(f"no row block for {rows} rows in {n_steps} steps")


def _attention(lq1, lk1, lq2, lk2, q, k, v_t, subln_g, norm_sq_max, f32_weights):
    seq = q.shape[0]
    tq = 1024
    tkc = 512
    n_q = seq // tq
    n_steps = N_DIFF_HEADS * n_q
    lam_spec = pl.BlockSpec((1, DIFF_HEAD_DIM), lambda h, i: (0, 0))
    cast_specs, cast_blocks = [], []
    for w in f32_weights:
        block = _cast_block_rows(w.shape[0], n_steps)
        n_blocks = w.shape[0] // block
        cast_blocks.append(n_blocks)
        cast_specs.append(pl.BlockSpec(
            (block, w.shape[1]), lambda h, i, last=n_blocks - 1: (jnp.minimum(h * n_q + i, last), 0)))
    outs = pl.pallas_call(
        functools.partial(_attn_kernel, tkc=tkc, cast_blocks=tuple(cast_blocks)),
        grid=(N_DIFF_HEADS, n_q),
        in_specs=[
            lam_spec, lam_spec, lam_spec, lam_spec,
            pl.BlockSpec((tq, DIFF_V_DIM), lambda h, i: (i, h)),
            pl.BlockSpec((seq, DIFF_V_DIM), lambda h, i: (0, h)),
            pl.BlockSpec((DIFF_V_DIM, seq), lambda h, i: (h, 0)),
            pl.BlockSpec((1, DIFF_V_DIM), lambda h, i: (0, 0)),
            pl.BlockSpec(norm_sq_max.shape, lambda h, i: (0, 0, 0)),
        ] + cast_specs,
        out_specs=[pl.BlockSpec((tq, DIFF_V_DIM), lambda h, i: (i, h))] + cast_specs,
        out_shape=[jax.ShapeDtypeStruct((seq, N_DIFF_HEADS * DIFF_V_DIM), BF16)]
        + [jax.ShapeDtypeStruct(w.shape, BF16) for w in f32_weights],
        scratch_shapes=[pltpu.VMEM((seq // tkc, tkc, 2 * tq), BF16),
                        pltpu.VMEM((8, 2 * tq), F32),
                        pltpu.VMEM((8, 2 * tq), F32),
                        pltpu.VMEM((DIFF_V_DIM, tq), F32)],
        compiler_params=_params(("arbitrary", "arbitrary")),
        name="diff_attention",
    )(lq1, lk1, lq2, lk2, q, k, v_t, subln_g, norm_sq_max, *f32_weights)
    return outs[0], outs[1:]


def _outproj_kernel(a_ref, gm_ref, w_ref, x_ref, gpost_ref, gffn_ref, x1_ref, h2_ref, *, row_block):
    tm, half = a_ref.shape
    for r in range(0, tm, row_block):
        rows = slice(r, r + row_block)
        mix = (jnp.dot(a_ref[rows, :], w_ref[0:half, :], preferred_element_type=F32)
               + jnp.dot(gm_ref[rows, :], w_ref[half:2 * half, :], preferred_element_type=F32))
        x1 = x_ref[rows, :] + _rms(mix, RMS_EPS) * gpost_ref[...]
        x1_ref[rows, :] = x1
        h2_ref[rows, :] = (_rms(x1, RMS_EPS) * gffn_ref[...]).astype(BF16)


def _outproj(attn, gm, w_out, x, post_mix_g, pre_ffn_g):
    seq, d_model = x.shape
    half = attn.shape[1]
    tm = 512
    row_spec = lambda cols: pl.BlockSpec((tm, cols), lambda i: (i, 0))
    const_spec = lambda shape: pl.BlockSpec(shape, lambda i: (0, 0))
    return pl.pallas_call(
        functools.partial(_outproj_kernel, row_block=128),
        grid=(seq // tm,),
        in_specs=[row_spec(half), row_spec(half), const_spec(w_out.shape), row_spec(d_model),
                  const_spec((1, d_model)), const_spec((1, d_model))],
        out_specs=[row_spec(d_model), row_spec(d_model)],
        out_shape=[jax.ShapeDtypeStruct((seq, d_model), F32), jax.ShapeDtypeStruct((seq, d_model), BF16)],
        compiler_params=_params(("arbitrary",)),
        name="outproj",
    )(attn, gm, w_out, x, post_mix_g, pre_ffn_g)


def _ffn_kernel(h_ref, wg_ref, wu_ref, wd_ref, x1_hbm, g_ref, o_ref, x1_buf, x1_sem):
    i = pl.program_id(0)
    j = pl.program_id(1)
    tm = o_ref.shape[0]
    x1_copy = pltpu.make_async_copy(x1_hbm.at[pl.ds(pl.multiple_of(i * tm, tm), tm), :], x1_buf, x1_sem)

    @pl.when(j == 0)
    def _():
        x1_copy.start()
        o_ref[...] = jnp.zeros_like(o_ref)

    h = h_ref[...]
    gate = jnp.dot(h, wg_ref[...], preferred_element_type=F32)
    up = jnp.dot(h, wu_ref[...], preferred_element_type=F32)
    act = (gate * jax.nn.sigmoid(gate) * up).astype(BF16)
    o_ref[...] += jnp.dot(act, wd_ref[...], preferred_element_type=F32)

    @pl.when(j == pl.num_programs(1) - 1)
    def _():
        x1_copy.wait()
        o_ref[...] = x1_buf[...] + _rms(o_ref[...], RMS_EPS) * g_ref[...]


def _ffn(h2, w_gate, w_up, w_down, x1, post_ffn_g):
    seq, d_model = x1.shape
    d_ff = w_gate.shape[1]
    tm = 1024
    tf = 512
    return pl.pallas_call(
        _ffn_kernel,
        grid=(seq // tm, d_ff // tf),
        in_specs=[
            pl.BlockSpec((tm, d_model), lambda i, j: (i, 0)),
            pl.BlockSpec((d_model, tf), lambda i, j: (0, j)),
            pl.BlockSpec((d_model, tf), lambda i, j: (0, j)),
            pl.BlockSpec((tf, d_model), lambda i, j: (j, 0)),
            pl.BlockSpec(memory_space=pl.ANY),
            pl.BlockSpec((1, d_model), lambda i, j: (0, 0)),
        ],
        out_specs=pl.BlockSpec((tm, d_model), lambda i, j: (i, 0)),
        out_shape=jax.ShapeDtypeStruct((seq, d_model), F32),
        scratch_shapes=[pltpu.VMEM((tm, d_model), F32), pltpu.SemaphoreType.DMA(())],
        compiler_params=_params(("arbitrary", "arbitrary")),
        name="ffn",
    )(h2, w_gate, w_up, w_down, x1, post_ffn_g)


def kernel(x, positions, pre_mix_g, w_in, lambda_q1, lambda_k1, lambda_q2, lambda_k2, subln_g,
           gmlp_ln_g, gmlp_ln_b, w_s, b_s, w_out, post_mix_g, pre_ffn_g, w_gate, w_up, w_down,
           post_ffn_g):
    batch, seq, d_model = x.shape
    assert batch == 1 and pre_mix_g.shape[0] == 1, "single sequence, single layer"
    assert seq % 1024 == 0 and d_model == 2048
    x2 = x[0]
    inv_freq = ROPE_THETA ** (-jnp.arange(0, DIFF_HEAD_DIM, 2, dtype=F32) / DIFF_HEAD_DIM)
    inv_lanes = jnp.tile(inv_freq, LANES // (DIFF_HEAD_DIM // 2))[None, :]
    bs_full = jnp.repeat(b_s[0].T, LANES, axis=1)
    q, k, v_t, gm, norm_sq_max = _inproj(x2, pre_mix_g, w_in[0].astype(BF16), positions[0][:, None], inv_lanes,
                          gmlp_ln_g, gmlp_ln_b, w_s[0].astype(BF16), bs_full)
    attn, (w_out16, w_gate16, w_up16, w_down16) = _attention(
        lambda_q1, lambda_k1, lambda_q2, lambda_k2, q, k, v_t, subln_g, norm_sq_max,
        (w_out[0], w_gate[0], w_up[0], w_down[0]))
    x1, h2 = _outproj(attn, gm, w_out16, x2, post_mix_g, pre_ffn_g)
    out = _ffn(h2, w_gate16, w_up16, w_down16, x1, post_ffn_g)
    return out[None]
```

```python
import functools
import math

import jax
import jax.numpy as jnp
from jax import lax
from jax.experimental import pallas as pl
from jax.experimental.pallas import tpu as pltpu

N_DIFF_HEADS = 8
DIFF_HEAD_DIM = 64
DIFF_V_DIM = 2 * DIFF_HEAD_DIM
N_GMLP_GROUPS = 8
CHUNK = 128
ROPE_THETA = 10000.0
RMS_EPS = 1e-6
LN_EPS = 1e-5
SUBLN_EPS = 1e-5
LAMBDA_INIT = 0.8 - 0.6 * math.exp(-0.3 * 0)

LANES = 128
VMEM_LIMIT_BYTES = 60 * 1024 * 1024

F32 = jnp.float32
BF16 = jnp.bfloat16


def _rms(x, eps):
    return x * lax.rsqrt(jnp.mean(x * x, axis=-1, keepdims=True) + eps)


def _gelu(x):
    return 0.5 * x * (1.0 + lax.erf(x * (1.0 / math.sqrt(2.0))))


def _params(semantics):
    return pltpu.CompilerParams(dimension_semantics=semantics, vmem_limit_bytes=VMEM_LIMIT_BYTES)


def _first_half_lanes(shape):
    lane = lax.broadcasted_iota(jnp.int32, shape, 1)
    return (lane % DIFF_HEAD_DIM) < (DIFF_HEAD_DIM // 2)


def _rope(x, cos, sin_signed):
    half = DIFF_HEAD_DIM // 2
    partner = jnp.where(_first_half_lanes(x.shape), pltpu.roll(x, LANES - half, 1), pltpu.roll(x, half, 1))
    return x * cos + partner * sin_signed


def _inproj_kernel(x_ref, g_ref, w_ref, pos_ref, inv_ref, lng_ref, lnb_ref, ws_ref, bs_ref,
                   q_ref, k_ref, vt_ref, gm_ref, nmax_ref, h_ref, gu_ref, *, q_scale):
    tm, width = q_ref.shape
    q_cols, k_cols, v_cols, gu_cols, gv_cols = (slice(n * width, (n + 1) * width) for n in range(5))
    def proj(cols):
        return jnp.dot(h_ref[...], w_ref[:, cols], preferred_element_type=F32)

    h_ref[...] = (_rms(x_ref[...], RMS_EPS) * g_ref[...]).astype(BF16)
    vt_ref[...] = proj(v_cols).T.astype(BF16)

    ang = pos_ref[...].astype(F32) * inv_ref[...]
    sin = jnp.sin(ang)
    sin = jnp.where(_first_half_lanes(ang.shape), -sin, sin)
    cos = jnp.cos(ang)

    map1_lanes = lax.broadcasted_iota(jnp.int32, (tm, LANES), 1) < DIFF_HEAD_DIM
    nmax_lane = lax.broadcasted_iota(jnp.int32, nmax_ref.shape[1:], 1)

    def rope_store(out_ref, cols, scale, nmax, first_lane):
        acc = proj(cols)
        for c in range(width // LANES):
            sl = slice(c * LANES, (c + 1) * LANES)
            rotated = (_rope(acc[:, sl], cos, sin) * scale).astype(BF16)
            out_ref[:, sl] = rotated
            sq = rotated.astype(F32)
            sq = sq * sq
            for m, in_map in enumerate((map1_lanes, jnp.logical_not(map1_lanes))):
                norm_sq = jnp.sum(jnp.where(in_map, sq, 0.0), axis=1, keepdims=True)
                nmax = jnp.where(nmax_lane == first_lane + 2 * c + m,
                                 jnp.max(norm_sq, axis=0, keepdims=True), nmax)
        return nmax

    nmax = rope_store(q_ref, q_cols, q_scale, jnp.zeros(nmax_ref.shape[1:], F32), 0)
    nmax_ref[0] = rope_store(k_ref, k_cols, 1.0, nmax, 2 * N_DIFF_HEADS)

    gu_ref[...] = _gelu(proj(gu_cols))
    gv = _gelu(proj(gv_cols))
    mu = jnp.mean(gv, axis=-1, keepdims=True)
    d = gv - mu
    var = jnp.mean(d * d, axis=-1, keepdims=True)
    vln = (d * lax.rsqrt(var + LN_EPS) * lng_ref[...] + lnb_ref[...]).astype(BF16)
    for c in range(tm // CHUNK):
        rows = slice(c * CHUNK, (c + 1) * CHUNK)
        for g in range(N_GMLP_GROUPS):
            cols = slice(g * LANES, (g + 1) * LANES)
            y = jnp.dot(ws_ref[g], vln[rows, cols], preferred_element_type=F32) + bs_ref[:, cols]
            gm_ref[rows, cols] = (gu_ref[rows, cols] * y).astype(BF16)


def _inproj(x, pre_g, w_in, positions, inv_lanes, ln_g, ln_b, w_s, bs_full):
    seq, d_model = x.shape
    width = w_in.shape[1] // 5
    tm = 512
    row_spec = lambda cols: pl.BlockSpec((tm, cols), lambda i: (i, 0))
    const_spec = lambda shape, **kw: pl.BlockSpec(shape, lambda i: (0,) * len(shape), **kw)
    return pl.pallas_call(
        functools.partial(_inproj_kernel, q_scale=DIFF_HEAD_DIM ** -0.5 * math.log2(math.e)),
        grid=(seq // tm,),
        in_specs=[
            row_spec(d_model),
            const_spec((1, d_model)),
            const_spec(w_in.shape, pipeline_mode=pl.Buffered(1)),
            row_spec(1),
            const_spec((1, LANES)),
            const_spec((1, width)),
            const_spec((1, width)),
            const_spec(w_s.shape),
            const_spec(bs_full.shape),
        ],
        out_specs=[row_spec(width), row_spec(width), pl.BlockSpec((width, tm), lambda i: (0, i)),
                   row_spec(width), pl.BlockSpec((1, 8, LANES), lambda i: (i, 0, 0))],
        out_shape=[jax.ShapeDtypeStruct((seq, width), BF16), jax.ShapeDtypeStruct((seq, width), BF16),
                   jax.ShapeDtypeStruct((width, seq), BF16), jax.ShapeDtypeStruct((seq, width), BF16),
                   jax.ShapeDtypeStruct((seq // tm, 8, LANES), F32)],
        scratch_shapes=[pltpu.VMEM((tm, d_model), BF16), pltpu.VMEM((tm, width), F32)],
        compiler_params=_params(("arbitrary",)),
        name="inproj",
    )(x, pre_g, w_in, positions, inv_lanes, ln_g, ln_b, w_s, bs_full)


def _attn_kernel(*refs, tkc, n_cast):
    lq1_ref, lk1_ref, lq2_ref, lk2_ref, q_ref, k_ref, vt_ref, g_ref, nmax_ref = refs[:9]
    cast_src = refs[9:9 + n_cast]
    o_ref = refs[9 + n_cast]
    cast_dst = refs[10 + n_cast:10 + 2 * n_cast]
    p_ref, shift_ref, lpart_ref, acc_ref = refs[10 + 2 * n_cast:]

    for src, dst in zip(cast_src, cast_dst):
        dst[...] = src[...].astype(BF16)

    tq = q_ref.shape[0]
    two_tq = 2 * tq
    nkc = k_ref.shape[0] // tkc
    lane = lax.broadcasted_iota(jnp.int32, (tq, LANES), 1)
    in_map1 = lane < DIFF_HEAD_DIM
    q = q_ref[...]
    zero = jnp.zeros_like(q)
    q_maps = jnp.concatenate([jnp.where(in_map1, q, zero), jnp.where(in_map1, zero, q)], axis=0)
    nt_dims = (((1,), (1,)), ((), ()))
    col_tiles = [slice(c, c + LANES) for c in range(0, two_tq, LANES)]

    def key_rows(kc):
        if isinstance(kc, int):
            return slice(kc * tkc, (kc + 1) * tkc)
        return pl.ds(pl.multiple_of(kc * tkc, tkc), tkc)

    def score_tiles(kc):
        s = lax.dot_general(k_ref[key_rows(kc), :], q_maps, nt_dims, preferred_element_type=F32)
        return [s[:, cols].reshape(tkc // 8, 8, LANES) for cols in col_tiles]

    def shifted_probs(kc):
        for cols, tile in zip(col_tiles, score_tiles(kc)):
            p = jnp.exp2(tile - shift_ref[:, cols])
            p_ref[kc, :, cols] = p.reshape(tkc, LANES).astype(BF16)
            lpart_ref[:, cols] += jnp.sum(p, axis=0)

    def column_max(kc):
        for cols, tile in zip(col_tiles, score_tiles(kc)):
            lpart_ref[:, cols] = jnp.maximum(lpart_ref[:, cols], jnp.max(tile, axis=0))

    def sublane_reduce(op):
        return op(lpart_ref[...], axis=0, keepdims=True)

    norm_sq_max = jnp.max(nmax_ref[...], axis=0)
    nmax_lane = lax.broadcasted_iota(jnp.int32, norm_sq_max.shape, 1)
    head_lane = 2 * pl.program_id(0)

    def norm_sq(lane_index):
        picked = jnp.where(nmax_lane == lane_index, norm_sq_max, 0.0)
        return jnp.max(jnp.max(picked, axis=1, keepdims=True), axis=0, keepdims=True)

    for mp in range(2):
        bound = jnp.sqrt(norm_sq(head_lane + mp) * norm_sq(2 * N_DIFF_HEADS + head_lane + mp))
        shift_ref[:, mp * tq:(mp + 1) * tq] = jnp.broadcast_to(bound, (8, tq))
    lpart_ref[...] = jnp.zeros_like(lpart_ref)
    for kc in range(nkc):
        shifted_probs(kc)
    bounded_sums = sublane_reduce(jnp.sum)

    def redo_with_column_max():
        lpart_ref[...] = jnp.full(lpart_ref.shape, -jnp.inf, F32)
        lax.fori_loop(0, nkc, lambda kc, c: (column_max(kc), c)[1], 0)
        shift_ref[...] = jnp.broadcast_to(sublane_reduce(jnp.max), shift_ref.shape)
        lpart_ref[...] = jnp.zeros_like(lpart_ref)
        lax.fori_loop(0, nkc, lambda kc, c: (shifted_probs(kc), c)[1], 0)
        return sublane_reduce(jnp.sum)

    sums = lax.cond(jnp.min(bounded_sums) >= 2.0 ** -60, lambda: bounded_sums, redo_with_column_max)
    l1 = sums[:, :tq]
    l2 = sums[:, tq:]

    lam = (jnp.exp(jnp.sum(lq1_ref[...] * lk1_ref[...], axis=1, keepdims=True))
           - jnp.exp(jnp.sum(lq2_ref[...] * lk2_ref[...], axis=1, keepdims=True)) + LAMBDA_INIT)
    ratio = jnp.broadcast_to(lam * l1 / l2, (BF16_SUBLANES, tq)).astype(BF16)
    acc_ref[...] = jnp.zeros_like(acc_ref)
    for kc in range(nkc):
        p1 = p_ref[kc, :, 0:tq].reshape(tkc // BF16_SUBLANES, BF16_SUBLANES, tq)
        p2 = p_ref[kc, :, tq:two_tq].reshape(tkc // BF16_SUBLANES, BF16_SUBLANES, tq)
        w = (p1 - p2 * ratio).reshape(tkc, tq)
        acc_ref[...] += jnp.dot(vt_ref[:, kc * tkc:(kc + 1) * tkc], w, preferred_element_type=F32)

    out_t = acc_ref[...] * (1.0 / l1)
    out_t = out_t * lax.rsqrt(jnp.mean(out_t * out_t, axis=0, keepdims=True) + SUBLN_EPS)
    o_ref[...] = (out_t.T * g_ref[...] * (1.0 - LAMBDA_INIT)).astype(BF16)


BF16_SUBLANES = 16


def _cast_block_rows(rows, n_steps):
    for block in range(BF16_SUBLANES, rows + 1, BF16_SUBLANES):
        if rows % block == 0 and rows // block <= n_steps:
            return block
    raise ValueError(f"no row block for {rows} rows in {n_steps} steps")


def _attention(lq1, lk1, lq2, lk2, q, k, v_t, subln_g, norm_sq_max, f32_weights):
    seq = q.shape[0]
    tq = 1024
    tkc = 512
    n_q = seq // tq
    n_steps = N_DIFF_HEADS * n_q
    lam_spec = pl.BlockSpec((1, DIFF_HEAD_DIM), lambda h, i: (0, 0))
    cast_specs = []
    for w in f32_weights:
        block = _cast_block_rows(w.shape[0], n_steps)
        n_blocks = w.shape[0] // block
        cast_specs.append(pl.BlockSpec(
            (block, w.shape[1]), lambda h, i, last=n_blocks - 1: (jnp.minimum(h * n_q + i, last), 0)))
    outs = pl.pallas_call(
        functools.partial(_attn_kernel, tkc=tkc, n_cast=len(f32_weights)),
        grid=(N_DIFF_HEADS, n_q),
        in_specs=[
            lam_spec, lam_spec, lam_spec, lam_spec,
            pl.BlockSpec((tq, DIFF_V_DIM), lambda h, i: (i, h)),
            pl.BlockSpec((seq, DIFF_V_DIM), lambda h, i: (0, h)),
            pl.BlockSpec((DIFF_V_DIM, seq), lambda h, i: (h, 0)),
            pl.BlockSpec((1, DIFF_V_DIM), lambda h, i: (0, 0)),
            pl.BlockSpec(norm_sq_max.shape, lambda h, i: (0, 0, 0)),
        ] + cast_specs,
        out_specs=[pl.BlockSpec((tq, DIFF_V_DIM), lambda h, i: (i, h))] + cast_specs,
        out_shape=[jax.ShapeDtypeStruct((seq, N_DIFF_HEADS * DIFF_V_DIM), BF16)]
        + [jax.ShapeDtypeStruct(w.shape, BF16) for w in f32_weights],
        scratch_shapes=[pltpu.VMEM((seq // tkc, tkc, 2 * tq), BF16),
                        pltpu.VMEM((8, 2 * tq), F32),
                        pltpu.VMEM((8, 2 * tq), F32),
                        pltpu.VMEM((DIFF_V_DIM, tq), F32)],
        compiler_params=_params(("arbitrary", "arbitrary")),
        name="diff_attention",
    )(lq1, lk1, lq2, lk2, q, k, v_t, subln_g, norm_sq_max, *f32_weights)
    return outs[0], outs[1:]


def _outproj_kernel(a_ref, gm_ref, w_ref, x_ref, gpost_ref, gffn_ref, x1_ref, h2_ref, *, row_block):
    tm, half = a_ref.shape
    for r in range(0, tm, row_block):
        rows = slice(r, r + row_block)
        mix = (jnp.dot(a_ref[rows, :], w_ref[0:half, :], preferred_element_type=F32)
               + jnp.dot(gm_ref[rows, :], w_ref[half:2 * half, :], preferred_element_type=F32))
        x1 = x_ref[rows, :] + _rms(mix, RMS_EPS) * gpost_ref[...]
        x1_ref[rows, :] = x1
        h2_ref[rows, :] = (_rms(x1, RMS_EPS) * gffn_ref[...]).astype(BF16)


def _outproj(attn, gm, w_out, x, post_mix_g, pre_ffn_g):
    seq, d_model = x.shape
    half = attn.shape[1]
    tm = 512
    row_spec = lambda cols: pl.BlockSpec((tm, cols), lambda i: (i, 0))
    const_spec = lambda shape: pl.BlockSpec(shape, lambda i: (0, 0))
    return pl.pallas_call(
        functools.partial(_outproj_kernel, row_block=128),
        grid=(seq // tm,),
        in_specs=[row_spec(half), row_spec(half), const_spec(w_out.shape), row_spec(d_model),
                  const_spec((1, d_model)), const_spec((1, d_model))],
        out_specs=[row_spec(d_model), row_spec(d_model)],
        out_shape=[jax.ShapeDtypeStruct((seq, d_model), F32), jax.ShapeDtypeStruct((seq, d_model), BF16)],
        compiler_params=_params(("arbitrary",)),
        name="outproj",
    )(attn, gm, w_out, x, post_mix_g, pre_ffn_g)


def _ffn_kernel(h_ref, wg_ref, wu_ref, wd_ref, x1_hbm, g_ref, o_ref, x1_buf, x1_sem):
    i = pl.program_id(0)
    j = pl.program_id(1)
    tm = o_ref.shape[0]
    x1_copy = pltpu.make_async_copy(x1_hbm.at[pl.ds(pl.multiple_of(i * tm, tm), tm), :], x1_buf, x1_sem)

    @pl.when(j == 0)
    def _():
        x1_copy.start()
        o_ref[...] = jnp.zeros_like(o_ref)

    h = h_ref[...]
    gate = jnp.dot(h, wg_ref[...], preferred_element_type=F32)
    up = jnp.dot(h, wu_ref[...], preferred_element_type=F32)
    act = (gate * jax.nn.sigmoid(gate) * up).astype(BF16)
    o_ref[...] += jnp.dot(act, wd_ref[...], preferred_element_type=F32)

    @pl.when(j == pl.num_programs(1) - 1)
    def _():
        x1_copy.wait()
        o_ref[...] = x1_buf[...] + _rms(o_ref[...], RMS_EPS) * g_ref[...]


def _ffn(h2, w_gate, w_up, w_down, x1, post_ffn_g):
    seq, d_model = x1.shape
    d_ff = w_gate.shape[1]
    tm = 1024
    tf = 512
    return pl.pallas_call(
        _ffn_kernel,
        grid=(seq // tm, d_ff // tf),
        in_specs=[
            pl.BlockSpec((tm, d_model), lambda i, j: (i, 0)),
            pl.BlockSpec((d_model, tf), lambda i, j: (0, j)),
            pl.BlockSpec((d_model, tf), lambda i, j: (0, j)),
            pl.BlockSpec((tf, d_model), lambda i, j: (j, 0)),
            pl.BlockSpec(memory_space=pl.ANY),
            pl.BlockSpec((1, d_model), lambda i, j: (0, 0)),
        ],
        out_specs=pl.BlockSpec((tm, d_model), lambda i, j: (i, 0)),
        out_shape=jax.ShapeDtypeStruct((seq, d_model), F32),
        scratch_shapes=[pltpu.VMEM((tm, d_model), F32), pltpu.SemaphoreType.DMA(())],
        compiler_params=_params(("arbitrary", "arbitrary")),
        name="ffn",
    )(h2, w_gate, w_up, w_down, x1, post_ffn_g)


def kernel(x, positions, pre_mix_g, w_in, lambda_q1, lambda_k1, lambda_q2, lambda_k2, subln_g,
           gmlp_ln_g, gmlp_ln_b, w_s, b_s, w_out, post_mix_g, pre_ffn_g, w_gate, w_up, w_down,
           post_ffn_g):
    batch, seq, d_model = x.shape
    assert batch == 1 and pre_mix_g.shape[0] == 1, "single sequence, single layer"
    assert seq % 1024 == 0 and d_model == 2048
    x2 = x[0]
    inv_freq = ROPE_THETA ** (-jnp.arange(0, DIFF_HEAD_DIM, 2, dtype=F32) / DIFF_HEAD_DIM)
    inv_lanes = jnp.tile(inv_freq, LANES // (DIFF_HEAD_DIM // 2))[None, :]
    bs_full = jnp.repeat(b_s[0].T, LANES, axis=1)
    q, k, v_t, gm, norm_sq_max = _inproj(x2, pre_mix_g, w_in[0].astype(BF16), positions[0][:, None], inv_lanes,
                          gmlp_ln_g, gmlp_ln_b, w_s[0].astype(BF16), bs_full)
    attn, (w_out16, w_gate16, w_up16, w_down16) = _attention(
        lambda_q1, lambda_k1, lambda_q2, lambda_k2, q, k, v_t, subln_g, norm_sq_max,
        (w_out[0], w_gate[0], w_up[0], w_down[0]))
    x1, h2 = _outproj(attn, gm, w_out16, x2, post_mix_g, pre_ffn_g)
    out = _ffn(h2, w_gate16, w_up16, w_down16, x1, post_ffn_g)
    return out[None]
```

```python
import functools
import math

import jax
import jax.numpy as jnp
from jax import lax
from jax.experimental import pallas as pl
from jax.experimental.pallas import tpu as pltpu

N_DIFF_HEADS = 8
DIFF_HEAD_DIM = 64
DIFF_V_DIM = 2 * DIFF_HEAD_DIM
N_GMLP_GROUPS = 8
CHUNK = 128
ROPE_THETA = 10000.0
RMS_EPS = 1e-6
LN_EPS = 1e-5
SUBLN_EPS = 1e-5
LAMBDA_INIT = 0.8 - 0.6 * math.exp(-0.3 * 0)

LANES = 128
VMEM_LIMIT_BYTES = 60 * 1024 * 1024

F32 = jnp.float32
BF16 = jnp.bfloat16


def _rms(x, eps):
    return x * lax.rsqrt(jnp.mean(x * x, axis=-1, keepdims=True) + eps)


def _gelu(x):
    return 0.5 * x * (1.0 + lax.erf(x * (1.0 / math.sqrt(2.0))))


def _params(semantics):
    return pltpu.CompilerParams(dimension_semantics=semantics, vmem_limit_bytes=VMEM_LIMIT_BYTES)


def _first_half_lanes(shape):
    lane = lax.broadcasted_iota(jnp.int32, shape, 1)
    return (lane % DIFF_HEAD_DIM) < (DIFF_HEAD_DIM // 2)


def _rope(x, cos, sin_signed):
    half = DIFF_HEAD_DIM // 2
    partner = jnp.where(_first_half_lanes(x.shape), pltpu.roll(x, LANES - half, 1), pltpu.roll(x, half, 1))
    return x * cos + partner * sin_signed


def _inproj_kernel(x_ref, g_ref, w_ref, pos_ref, inv_ref, lng_ref, lnb_ref, ws_ref, bs_ref,
                   q_ref, k_ref, vt_ref, gm_ref, nmax_ref, h_ref, gu_ref, *, q_scale):
    tm, width = q_ref.shape
    q_cols, k_cols, v_cols, gu_cols, gv_cols = (slice(n * width, (n + 1) * width) for n in range(5))
    def proj(cols):
        return jnp.dot(h_ref[...], w_ref[:, cols], preferred_element_type=F32)

    h_ref[...] = (_rms(x_ref[...], RMS_EPS) * g_ref[...]).astype(BF16)
    vt_ref[...] = proj(v_cols).T.astype(BF16)

    ang = pos_ref[...].astype(F32) * inv_ref[...]
    sin = jnp.sin(ang)
    sin = jnp.where(_first_half_lanes(ang.shape), -sin, sin)
    cos = jnp.cos(ang)

    map1_lanes = lax.broadcasted_iota(jnp.int32, (tm, LANES), 1) < DIFF_HEAD_DIM
    nmax_lane = lax.broadcasted_iota(jnp.int32, nmax_ref.shape[1:], 1)

    def rope_store(out_ref, cols, scale, nmax, first_lane):
        acc = proj(cols)
        for c in range(width // LANES):
            sl = slice(c * LANES, (c + 1) * LANES)
            rotated = (_rope(acc[:, sl], cos, sin) * scale).astype(BF16)
            out_ref[:, sl] = rotated
            sq = rotated.astype(F32)
            sq = sq * sq
            for m, in_map in enumerate((map1_lanes, jnp.logical_not(map1_lanes))):
                norm_sq = jnp.sum(jnp.where(in_map, sq, 0.0), axis=1, keepdims=True)
                nmax = jnp.where(nmax_lane == first_lane + 2 * c + m,
                                 jnp.max(norm_sq, axis=0, keepdims=True), nmax)
        return nmax

    nmax = rope_store(q_ref, q_cols, q_scale, jnp.zeros(nmax_ref.shape[1:], F32), 0)
    nmax_ref[0] = rope_store(k_ref, k_cols, 1.0, nmax, 2 * N_DIFF_HEADS)

    gu_ref[...] = _gelu(proj(gu_cols))
    gv = _gelu(proj(gv_cols))
    mu = jnp.mean(gv, axis=-1, keepdims=True)
    d = gv - mu
    var = jnp.mean(d * d, axis=-1, keepdims=True)
    vln = (d * lax.rsqrt(var + LN_EPS) * lng_ref[...] + lnb_ref[...]).astype(BF16)
    for c in range(tm // CHUNK):
        rows = slice(c * CHUNK, (c + 1) * CHUNK)
        for g in range(N_GMLP_GROUPS):
            cols = slice(g * LANES, (g + 1) * LANES)
            y = jnp.dot(ws_ref[g], vln[rows, cols], preferred_element_type=F32) + bs_ref[:, cols]
            gm_ref[rows, cols] = (gu_ref[rows, cols] * y).astype(BF16)


def _inproj(x, pre_g, w_in, positions, inv_lanes, ln_g, ln_b, w_s, bs_full):
    seq, d_model = x.shape
    width = w_in.shape[1] // 5
    tm = 512
    row_spec = lambda cols: pl.BlockSpec((tm, cols), lambda i: (i, 0))
    const_spec = lambda shape, **kw: pl.BlockSpec(shape, lambda i: (0,) * len(shape), **kw)
    return pl.pallas_call(
        functools.partial(_inproj_kernel, q_scale=DIFF_HEAD_DIM ** -0.5 * math.log2(math.e)),
        grid=(seq // tm,),
        in_specs=[
            row_spec(d_model),
            const_spec((1, d_model)),
            const_spec(w_in.shape, pipeline_mode=pl.Buffered(1)),
            row_spec(1),
            const_spec((1, LANES)),
            const_spec((1, width)),
            const_spec((1, width)),
            const_spec(w_s.shape),
            const_spec(bs_full.shape),
        ],
        out_specs=[row_spec(width), row_spec(width), pl.BlockSpec((width, tm), lambda i: (0, i)),
                   row_spec(width), pl.BlockSpec((1, 8, LANES), lambda i: (i, 0, 0))],
        out_shape=[jax.ShapeDtypeStruct((seq, width), BF16), jax.ShapeDtypeStruct((seq, width), BF16),
                   jax.ShapeDtypeStruct((width, seq), BF16), jax.ShapeDtypeStruct((seq, width), BF16),
                   jax.ShapeDtypeStruct((seq // tm, 8, LANES), F32)],
        scratch_shapes=[pltpu.VMEM((tm, d_model), BF16), pltpu.VMEM((tm, width), F32)],
        compiler_params=_params(("arbitrary",)),
        name="inproj",
    )(x, pre_g, w_in, positions, inv_lanes, ln_g, ln_b, w_s, bs_full)


def _attn_kernel(*refs, tkc, n_cast):
    lq1_ref, lk1_ref, lq2_ref, lk2_ref, q_ref, k_ref, vt_ref, g_ref, nmax_ref = refs[:9]
    cast_src = refs[9:9 + n_cast]
    o_ref = refs[9 + n_cast]
    cast_dst = refs[10 + n_cast:10 + 2 * n_cast]
    p_ref, shift_ref, lpart_ref, acc_ref = refs[10 + 2 * n_cast:]

    for src, dst in zip(cast_src, cast_dst):
        dst[...] = src[...].astype(BF16)

    tq = q_ref.shape[0]
    two_tq = 2 * tq
    nkc = k_ref.shape[0] // tkc
    lane = lax.broadcasted_iota(jnp.int32, (tq, LANES), 1)
    in_map1 = lane < DIFF_HEAD_DIM
    q = q_ref[...]
    zero = jnp.zeros_like(q)
    q_maps = jnp.concatenate([jnp.where(in_map1, q, zero), jnp.where(in_map1, zero, q)], axis=0)
    nt_dims = (((1,), (1,)), ((), ()))
    col_tiles = [slice(c, c + LANES) for c in range(0, two_tq, LANES)]

    def key_rows(kc):
        if isinstance(kc, int):
            return slice(kc * tkc, (kc + 1) * tkc)
        return pl.ds(pl.multiple_of(kc * tkc, tkc), tkc)

    def score_tiles(kc):
        s = lax.dot_general(k_ref[key_rows(kc), :], q_maps, nt_dims, preferred_element_type=F32)
        return [s[:, cols].reshape(tkc // 8, 8, LANES) for cols in col_tiles]

    def shifted_probs(kc):
        for cols, tile in zip(col_tiles, score_tiles(kc)):
            p = jnp.exp2(tile - shift_ref[:, cols])
            p_ref[kc, :, cols] = p.reshape(tkc, LANES).astype(BF16)
            lpart_ref[:, cols] += jnp.sum(p, axis=0)

    def column_max(kc):
        for cols, tile in zip(col_tiles, score_tiles(kc)):
            lpart_ref[:, cols] = jnp.maximum(lpart_ref[:, cols], jnp.max(tile, axis=0))

    def sublane_reduce(op):
        return op(lpart_ref[...], axis=0, keepdims=True)

    norm_sq_max = jnp.max(nmax_ref[...], axis=0)
    nmax_lane = lax.broadcasted_iota(jnp.int32, norm_sq_max.shape, 1)
    head_lane = 2 * pl.program_id(0)

    def norm_sq(lane_index):
        picked = jnp.where(nmax_lane == lane_index, norm_sq_max, 0.0)
        return jnp.max(jnp.max(picked, axis=1, keepdims=True), axis=0, keepdims=True)

    for mp in range(2):
        bound = jnp.sqrt(norm_sq(head_lane + mp) * norm_sq(2 * N_DIFF_HEADS + head_lane + mp))
        shift_ref[:, mp * tq:(mp + 1) * tq] = jnp.broadcast_to(bound, (8, tq))
    lpart_ref[...] = jnp.zeros_like(lpart_ref)
    for kc in range(nkc):
        shifted_probs(kc)
    bounded_sums = sublane_reduce(jnp.sum)

    def redo_with_column_max():
        lpart_ref[...] = jnp.full(lpart_ref.shape, -jnp.inf, F32)
        lax.fori_loop(0, nkc, lambda kc, c: (column_max(kc), c)[1], 0)
        shift_ref[...] = jnp.broadcast_to(sublane_reduce(jnp.max), shift_ref.shape)
        lpart_ref[...] = jnp.zeros_like(lpart_ref)
        lax.fori_loop(0, nkc, lambda kc, c: (shifted_probs(kc), c)[1], 0)
        return sublane_reduce(jnp.sum)

    sums = lax.cond(jnp.min(bounded_sums) >= 2.0 ** -60, lambda: bounded_sums, redo_with_column_max)
    l1 = sums[:, :tq]
    l2 = sums[:, tq:]

    lam = (jnp.exp(jnp.sum(lq1_ref[...] * lk1_ref[...], axis=1, keepdims=True))
           - jnp.exp(jnp.sum(lq2_ref[...] * lk2_ref[...], axis=1, keepdims=True)) + LAMBDA_INIT)
    ratio = jnp.broadcast_to(lam * l1 / l2, (BF16_SUBLANES, tq)).astype(BF16)
    acc_ref[...] = jnp.zeros_like(acc_ref)
    for kc in range(nkc):
        p1 = p_ref[kc, :, 0:tq].reshape(tkc // BF16_SUBLANES, BF16_SUBLANES, tq)
        p2 = p_ref[kc, :, tq:two_tq].reshape(tkc // BF16_SUBLANES, BF16_SUBLANES, tq)
        w = (p1 - p2 * ratio).reshape(tkc, tq)
        acc_ref[...] += jnp.dot(vt_ref[:, kc * tkc:(kc + 1) * tkc], w, preferred_element_type=F32)

    out_t = acc_ref[...] * (1.0 / l1)
    out_t = out_t * lax.rsqrt(jnp.mean(out_t * out_t, axis=0, keepdims=True) + SUBLN_EPS)
    o_ref[...] = (out_t.T * g_ref[...] * (1.0 - LAMBDA_INIT)).astype(BF16)


BF16_SUBLANES = 16


def _cast_block_rows(rows, n_steps):
    for block in range(BF16_SUBLANES, rows + 1, BF16_SUBLANES):
        if rows % block == 0 and rows // block <= n_steps:
            return block
    raise ValueError(f"no row block for {rows} rows in {n_steps} steps")


def _attention(lq1, lk1, lq2, lk2, q, k, v_t, subln_g, norm_sq_max, f32_weights):
    seq = q.shape[0]
    tq = 1024
    tkc = 512
    n_q = seq // tq
    n_steps = N_DIFF_HEADS * n_q
    lam_spec = pl.BlockSpec((1, DIFF_HEAD_DIM), lambda h, i: (0, 0))
    cast_specs = []
    for w in f32_weights:
        block = _cast_block_rows(w.shape[0], n_steps)
        n_blocks = w.shape[0] // block
        cast_specs.append(pl.BlockSpec(
            (block, w.shape[1]), lambda h, i, last=n_blocks - 1: (jnp.minimum(h * n_q + i, last), 0)))
    outs = pl.pallas_call(
        functools.partial(_attn_kernel, tkc=tkc, n_cast=len(f32_weights)),
        grid=(N_DIFF_HEADS, n_q),
        in_specs=[
            lam_spec, lam_spec, lam_spec, lam_spec,
            pl.BlockSpec((tq, DIFF_V_DIM), lambda h, i: (i, h)),
            pl.BlockSpec((seq, DIFF_V_DIM), lambda h, i: (0, h)),
            pl.BlockSpec((DIFF_V_DIM, seq), lambda h, i: (h, 0)),
            pl.BlockSpec((1, DIFF_V_DIM), lambda h, i: (0, 0)),
            pl.BlockSpec(norm_sq_max.shape, lambda h, i: (0, 0, 0)),
        ] + cast_specs,
        out_specs=[pl.BlockSpec((tq, DIFF_V_DIM), lambda h, i: (i, h))] + cast_specs,
        out_shape=[jax.ShapeDtypeStruct((seq, N_DIFF_HEADS * DIFF_V_DIM), BF16)]
        + [jax.ShapeDtypeStruct(w.shape, BF16) for w in f32_weights],
        scratch_shapes=[pltpu.VMEM((seq // tkc, tkc, 2 * tq), BF16),
                        pltpu.VMEM((8, 2 * tq), F32),
                        pltpu.VMEM((8, 2 * tq), F32),
                        pltpu.VMEM((DIFF_V_DIM, tq), F32)],
        compiler_params=_params(("arbitrary", "arbitrary")),
        name="diff_attention",
    )(lq1, lk1, lq2, lk2, q, k, v_t, subln_g, norm_sq_max, *f32_weights)
    return outs[0], outs[1:]


def _outproj_kernel(a_ref, gm_ref, w_ref, x_ref, gpost_ref, gffn_ref, x1_ref, h2_ref, *, row_block):
    tm, half = a_ref.shape
    for r in range(0, tm, row_block):
        rows = slice(r, r + row_block)
        mix = (jnp.dot(a_ref[rows, :], w_ref[0:half, :], preferred_element_type=F32)
               + jnp.dot(gm_ref[rows, :], w_ref[half:2 * half, :], preferred_element_type=F32))
        x1 = x_ref[rows, :] + _rms(mix, RMS_EPS) * gpost_ref[...]
        x1_ref[rows, :] = x1
        h2_ref[rows, :] = (_rms(x1, RMS_EPS) * gffn_ref[...]).astype(BF16)


def _outproj(attn, gm, w_out, x, post_mix_g, pre_ffn_g):
    seq, d_model = x.shape
    half = attn.shape[1]
    tm = 512
    row_spec = lambda cols: pl.BlockSpec((tm, cols), lambda i: (i, 0))
    const_spec = lambda shape: pl.BlockSpec(shape, lambda i: (0, 0))
    return pl.pallas_call(
        functools.partial(_outproj_kernel, row_block=128),
        grid=(seq // tm,),
        in_specs=[row_spec(half), row_spec(half), const_spec(w_out.shape), row_spec(d_model),
                  const_spec((1, d_model)), const_spec((1, d_model))],
        out_specs=[row_spec(d_model), row_spec(d_model)],
        out_shape=[jax.ShapeDtypeStruct((seq, d_model), F32), jax.ShapeDtypeStruct((seq, d_model), BF16)],
        compiler_params=_params(("arbitrary",)),
        name="outproj",
    )(attn, gm, w_out, x, post_mix_g, pre_ffn_g)


def _ffn_kernel(h_ref, wg_ref, wu_ref, wd_ref, x1_hbm, g_ref, o_ref, x1_buf, x1_sem):
    i = pl.program_id(0)
    j = pl.program_id(1)
    tm = o_ref.shape[0]
    x1_copy = pltpu.make_async_copy(x1_hbm.at[pl.ds(pl.multiple_of(i * tm, tm), tm), :], x1_buf, x1_sem)

    @pl.when(j == 0)
    def _():
        x1_copy.start()
        o_ref[...] = jnp.zeros_like(o_ref)

    def accumulate(rows):
        h = h_ref[rows, :]
        gate = jnp.dot(h, wg_ref[...], preferred_element_type=F32)
        up = jnp.dot(h, wu_ref[...], preferred_element_type=F32)
        act = (gate * jax.nn.sigmoid(gate) * up).astype(BF16)
        o_ref[rows, :] += jnp.dot(act, wd_ref[...], preferred_element_type=F32)

    last = pl.num_programs(1) - 1

    @pl.when(j < last)
    def _():
        accumulate(slice(0, tm))

    @pl.when(j == last)
    def _():
        x1_copy.wait()
        for r in range(0, tm, tm // 2):
            rows = slice(r, r + tm // 2)
            accumulate(rows)
            o_ref[rows, :] = x1_buf[rows, :] + _rms(o_ref[rows, :], RMS_EPS) * g_ref[...]


def _ffn(h2, w_gate, w_up, w_down, x1, post_ffn_g):
    seq, d_model = x1.shape
    d_ff = w_gate.shape[1]
    tm = 1024
    tf = 512
    return pl.pallas_call(
        _ffn_kernel,
        grid=(seq // tm, d_ff // tf),
        in_specs=[
            pl.BlockSpec((tm, d_model), lambda i, j: (i, 0)),
            pl.BlockSpec((d_model, tf), lambda i, j: (0, j)),
            pl.BlockSpec((d_model, tf), lambda i, j: (0, j)),
            pl.BlockSpec((tf, d_model), lambda i, j: (j, 0)),
            pl.BlockSpec(memory_space=pl.ANY),
            pl.BlockSpec((1, d_model), lambda i, j: (0, 0)),
        ],
        out_specs=pl.BlockSpec((tm, d_model), lambda i, j: (i, 0)),
        out_shape=jax.ShapeDtypeStruct((seq, d_model), F32),
        scratch_shapes=[pltpu.VMEM((tm, d_model), F32), pltpu.SemaphoreType.DMA(())],
        compiler_params=_params(("arbitrary", "arbitrary")),
        name="ffn",
    )(h2, w_gate, w_up, w_down, x1, post_ffn_g)


def kernel(x, positions, pre_mix_g, w_in, lambda_q1, lambda_k1, lambda_q2, lambda_k2, subln_g,
           gmlp_ln_g, gmlp_ln_b, w_s, b_s, w_out, post_mix_g, pre_ffn_g, w_gate, w_up, w_down,
           post_ffn_g):
    batch, seq, d_model = x.shape
    assert batch == 1 and pre_mix_g.shape[0] == 1, "single sequence, single layer"
    assert seq % 1024 == 0 and d_model == 2048
    x2 = x[0]
    inv_freq = ROPE_THETA ** (-jnp.arange(0, DIFF_HEAD_DIM, 2, dtype=F32) / DIFF_HEAD_DIM)
    inv_lanes = jnp.tile(inv_freq, LANES // (DIFF_HEAD_DIM // 2))[None, :]
    bs_full = jnp.repeat(b_s[0].T, LANES, axis=1)
    q, k, v_t, gm, norm_sq_max = _inproj(x2, pre_mix_g, w_in[0].astype(BF16), positions[0][:, None], inv_lanes,
                          gmlp_ln_g, gmlp_ln_b, w_s[0].astype(BF16), bs_full)
    attn, (w_out16, w_gate16, w_up16, w_down16) = _attention(
        lambda_q1, lambda_k1, lambda_q2, lambda_k2, q, k, v_t, subln_g, norm_sq_max,
        (w_out[0], w_gate[0], w_up[0], w_down[0]))
    x1, h2 = _outproj(attn, gm, w_out16, x2, post_mix_g, pre_ffn_g)
    out = _ffn(h2, w_gate16, w_up16, w_down16, x1, post_ffn_g)
    return out[None]
```

```python
import functools
import math

import jax
import jax.numpy as jnp
from jax import lax
from jax.experimental import pallas as pl
from jax.experimental.pallas import tpu as pltpu

N_DIFF_HEADS = 8
DIFF_HEAD_DIM = 64
DIFF_V_DIM = 2 * DIFF_HEAD_DIM
N_GMLP_GROUPS = 8
CHUNK = 128
ROPE_THETA = 10000.0
RMS_EPS = 1e-6
LN_EPS = 1e-5
SUBLN_EPS = 1e-5
LAMBDA_INIT = 0.8 - 0.6 * math.exp(-0.3 * 0)

LANES = 128
VMEM_LIMIT_BYTES = 60 * 1024 * 1024

F32 = jnp.float32
BF16 = jnp.bfloat16


def _rms(x, eps):
    return x * lax.rsqrt(jnp.mean(x * x, axis=-1, keepdims=True) + eps)


def _gelu(x):
    return 0.5 * x * (1.0 + lax.erf(x * (1.0 / math.sqrt(2.0))))


def _params(semantics):
    return pltpu.CompilerParams(dimension_semantics=semantics, vmem_limit_bytes=VMEM_LIMIT_BYTES)


def _first_half_lanes(shape):
    lane = lax.broadcasted_iota(jnp.int32, shape, 1)
    return (lane % DIFF_HEAD_DIM) < (DIFF_HEAD_DIM // 2)


def _rope(x, cos, sin_signed):
    half = DIFF_HEAD_DIM // 2
    partner = jnp.where(_first_half_lanes(x.shape), pltpu.roll(x, LANES - half, 1), pltpu.roll(x, half, 1))
    return x * cos + partner * sin_signed


def _inproj_kernel(x_ref, g_ref, w_ref, pos_ref, inv_ref, lng_ref, lnb_ref, ws_ref, bs_ref,
                   q_ref, k_ref, vt_ref, gm_ref, nmax_ref, h_ref, gu_ref, *, q_scale):
    tm, width = q_ref.shape
    q_cols, k_cols, v_cols, gu_cols, gv_cols = (slice(n * width, (n + 1) * width) for n in range(5))
    def proj(cols):
        return jnp.dot(h_ref[...], w_ref[:, cols], preferred_element_type=F32)

    h_ref[...] = (_rms(x_ref[...], RMS_EPS) * g_ref[...]).astype(BF16)
    vt_ref[...] = proj(v_cols).T.astype(BF16)

    ang = pos_ref[...].astype(F32) * inv_ref[...]
    sin = jnp.sin(ang)
    sin = jnp.where(_first_half_lanes(ang.shape), -sin, sin)
    cos = jnp.cos(ang)

    map1_lanes = lax.broadcasted_iota(jnp.int32, (tm, LANES), 1) < DIFF_HEAD_DIM
    nmax_lane = lax.broadcasted_iota(jnp.int32, nmax_ref.shape[1:], 1)

    def rope_store(out_ref, cols, scale, nmax, first_lane):
        acc = proj(cols)
        for c in range(width // LANES):
            sl = slice(c * LANES, (c + 1) * LANES)
            rotated = (_rope(acc[:, sl], cos, sin) * scale).astype(BF16)
            out_ref[:, sl] = rotated
            sq = rotated.astype(F32)
            sq = sq * sq
            for m, in_map in enumerate((map1_lanes, jnp.logical_not(map1_lanes))):
                norm_sq = jnp.sum(jnp.where(in_map, sq, 0.0), axis=1, keepdims=True)
                nmax = jnp.where(nmax_lane == first_lane + 2 * c + m,
                                 jnp.max(norm_sq, axis=0, keepdims=True), nmax)
        return nmax

    nmax = rope_store(q_ref, q_cols, q_scale, jnp.zeros(nmax_ref.shape[1:], F32), 0)
    nmax_ref[0] = rope_store(k_ref, k_cols, 1.0, nmax, 2 * N_DIFF_HEADS)

    gu_ref[...] = _gelu(proj(gu_cols))
    gv = _gelu(proj(gv_cols))
    mu = jnp.mean(gv, axis=-1, keepdims=True)
    d = gv - mu
    var = jnp.mean(d * d, axis=-1, keepdims=True)
    vln = (d * lax.rsqrt(var + LN_EPS) * lng_ref[...] + lnb_ref[...]).astype(BF16)
    for c in range(tm // CHUNK):
        rows = slice(c * CHUNK, (c + 1) * CHUNK)
        for g in range(N_GMLP_GROUPS):
            cols = slice(g * LANES, (g + 1) * LANES)
            y = jnp.dot(ws_ref[g], vln[rows, cols], preferred_element_type=F32) + bs_ref[:, cols]
            gm_ref[rows, cols] = (gu_ref[rows, cols] * y).astype(BF16)


def _inproj(x, pre_g, w_in, positions, inv_lanes, ln_g, ln_b, w_s, bs_full):
    seq, d_model = x.shape
    width = w_in.shape[1] // 5
    tm = 512
    row_spec = lambda cols: pl.BlockSpec((tm, cols), lambda i: (i, 0))
    const_spec = lambda shape, **kw: pl.BlockSpec(shape, lambda i: (0,) * len(shape), **kw)
    return pl.pallas_call(
        functools.partial(_inproj_kernel, q_scale=DIFF_HEAD_DIM ** -0.5 * math.log2(math.e)),
        grid=(seq // tm,),
        in_specs=[
            row_spec(d_model),
            const_spec((1, d_model)),
            const_spec(w_in.shape, pipeline_mode=pl.Buffered(1)),
            row_spec(1),
            const_spec((1, LANES)),
            const_spec((1, width)),
            const_spec((1, width)),
            const_spec(w_s.shape),
            const_spec(bs_full.shape),
        ],
        out_specs=[row_spec(width), row_spec(width), pl.BlockSpec((width, tm), lambda i: (0, i)),
                   row_spec(width), pl.BlockSpec((1, 8, LANES), lambda i: (i, 0, 0))],
        out_shape=[jax.ShapeDtypeStruct((seq, width), BF16), jax.ShapeDtypeStruct((seq, width), BF16),
                   jax.ShapeDtypeStruct((width, seq), BF16), jax.ShapeDtypeStruct((seq, width), BF16),
                   jax.ShapeDtypeStruct((seq // tm, 8, LANES), F32)],
        scratch_shapes=[pltpu.VMEM((tm, d_model), BF16), pltpu.VMEM((tm, width), F32)],
        compiler_params=_params(("arbitrary",)),
        name="inproj",
    )(x, pre_g, w_in, positions, inv_lanes, ln_g, ln_b, w_s, bs_full)


def _attn_kernel(*refs, tkc, n_cast):
    lq1_ref, lk1_ref, lq2_ref, lk2_ref, q_ref, k_ref, vt_ref, g_ref, nmax_ref = refs[:9]
    cast_src = refs[9:9 + n_cast]
    o_ref = refs[9 + n_cast]
    cast_dst = refs[10 + n_cast:10 + 2 * n_cast]
    p_ref, shift_ref, lpart_ref, acc_ref = refs[10 + 2 * n_cast:]

    for src, dst in zip(cast_src, cast_dst):
        dst[...] = src[...].astype(BF16)

    tq = q_ref.shape[0]
    two_tq = 2 * tq
    nkc = k_ref.shape[0] // tkc
    lane = lax.broadcasted_iota(jnp.int32, (tq, LANES), 1)
    in_map1 = lane < DIFF_HEAD_DIM
    q = q_ref[...]
    zero = jnp.zeros_like(q)
    q_maps = jnp.concatenate([jnp.where(in_map1, q, zero), jnp.where(in_map1, zero, q)], axis=0)
    nt_dims = (((1,), (1,)), ((), ()))
    col_tiles = [slice(c, c + LANES) for c in range(0, two_tq, LANES)]

    def key_rows(kc):
        if isinstance(kc, int):
            return slice(kc * tkc, (kc + 1) * tkc)
        return pl.ds(pl.multiple_of(kc * tkc, tkc), tkc)

    def score_tiles(kc):
        s = lax.dot_general(k_ref[key_rows(kc), :], q_maps, nt_dims, preferred_element_type=F32)
        return [s[:, cols].reshape(tkc // 8, 8, LANES) for cols in col_tiles]

    def shifted_probs(kc):
        for cols, tile in zip(col_tiles, score_tiles(kc)):
            p = jnp.exp2(tile - shift_ref[:, cols])
            p_ref[kc, :, cols] = p.reshape(tkc, LANES).astype(BF16)
            lpart_ref[:, cols] += jnp.sum(p, axis=0)

    def column_max(kc):
        for cols, tile in zip(col_tiles, score_tiles(kc)):
            lpart_ref[:, cols] = jnp.maximum(lpart_ref[:, cols], jnp.max(tile, axis=0))

    def sublane_reduce(op):
        return op(lpart_ref[...], axis=0, keepdims=True)

    norm_sq_max = jnp.max(nmax_ref[...], axis=0)
    nmax_lane = lax.broadcasted_iota(jnp.int32, norm_sq_max.shape, 1)
    head_lane = 2 * pl.program_id(0)

    def norm_sq(lane_index):
        picked = jnp.where(nmax_lane == lane_index, norm_sq_max, 0.0)
        return jnp.max(jnp.max(picked, axis=1, keepdims=True), axis=0, keepdims=True)

    for mp in range(2):
        bound = jnp.sqrt(norm_sq(head_lane + mp) * norm_sq(2 * N_DIFF_HEADS + head_lane + mp))
        shift_ref[:, mp * tq:(mp + 1) * tq] = jnp.broadcast_to(bound, (8, tq))
    lpart_ref[...] = jnp.zeros_like(lpart_ref)
    for kc in range(nkc):
        shifted_probs(kc)
    bounded_sums = sublane_reduce(jnp.sum)

    def redo_with_column_max():
        lpart_ref[...] = jnp.full(lpart_ref.shape, -jnp.inf, F32)
        lax.fori_loop(0, nkc, lambda kc, c: (column_max(kc), c)[1], 0)
        shift_ref[...] = jnp.broadcast_to(sublane_reduce(jnp.max), shift_ref.shape)
        lpart_ref[...] = jnp.zeros_like(lpart_ref)
        lax.fori_loop(0, nkc, lambda kc, c: (shifted_probs(kc), c)[1], 0)
        return sublane_reduce(jnp.sum)

    sums = lax.cond(jnp.min(bounded_sums) >= 2.0 ** -60, lambda: bounded_sums, redo_with_column_max)
    l1 = sums[:, :tq]
    l2 = sums[:, tq:]

    lam = (jnp.exp(jnp.sum(lq1_ref[...] * lk1_ref[...], axis=1, keepdims=True))
           - jnp.exp(jnp.sum(lq2_ref[...] * lk2_ref[...], axis=1, keepdims=True)) + LAMBDA_INIT)
    ratio = jnp.broadcast_to(lam * l1 / l2, (BF16_SUBLANES, tq)).astype(BF16)
    acc_ref[...] = jnp.zeros_like(acc_ref)
    for kc in range(nkc):
        p1 = p_ref[kc, :, 0:tq].reshape(tkc // BF16_SUBLANES, BF16_SUBLANES, tq)
        p2 = p_ref[kc, :, tq:two_tq].reshape(tkc // BF16_SUBLANES, BF16_SUBLANES, tq)
        w = (p1 - p2 * ratio).reshape(tkc, tq)
        acc_ref[...] += jnp.dot(vt_ref[:, kc * tkc:(kc + 1) * tkc], w, preferred_element_type=F32)

    out_t = acc_ref[...] * (1.0 / l1)
    out_t = out_t * lax.rsqrt(jnp.mean(out_t * out_t, axis=0, keepdims=True) + SUBLN_EPS)
    o_ref[...] = (out_t.T * g_ref[...] * (1.0 - LAMBDA_INIT)).astype(BF16)


BF16_SUBLANES = 16


def _cast_block_rows(rows, n_steps):
    for block in range(BF16_SUBLANES, rows + 1, BF16_SUBLANES):
        if rows % block == 0 and rows // block <= n_steps:
            return block
    raise ValueError(f"no row block for {rows} rows in {n_steps} steps")


def _attention(lq1, lk1, lq2, lk2, q, k, v_t, subln_g, norm_sq_max, f32_weights):
    seq = q.shape[0]
    tq = 1024
    tkc = 512
    n_q = seq // tq
    n_steps = N_DIFF_HEADS * n_q
    lam_spec = pl.BlockSpec((1, DIFF_HEAD_DIM), lambda h, i: (0, 0))
    cast_specs = []
    for w in f32_weights:
        block = _cast_block_rows(w.shape[0], n_steps)
        n_blocks = w.shape[0] // block
        cast_specs.append(pl.BlockSpec(
            (block, w.shape[1]), lambda h, i, last=n_blocks - 1: (jnp.minimum(h * n_q + i, last), 0)))
    outs = pl.pallas_call(
        functools.partial(_attn_kernel, tkc=tkc, n_cast=len(f32_weights)),
        grid=(N_DIFF_HEADS, n_q),
        in_specs=[
            lam_spec, lam_spec, lam_spec, lam_spec,
            pl.BlockSpec((tq, DIFF_V_DIM), lambda h, i: (i, h)),
            pl.BlockSpec((seq, DIFF_V_DIM), lambda h, i: (0, h)),
            pl.BlockSpec((DIFF_V_DIM, seq), lambda h, i: (h, 0)),
            pl.BlockSpec((1, DIFF_V_DIM), lambda h, i: (0, 0)),
            pl.BlockSpec(norm_sq_max.shape, lambda h, i: (0, 0, 0)),
        ] + cast_specs,
        out_specs=[pl.BlockSpec((tq, DIFF_V_DIM), lambda h, i: (i, h))] + cast_specs,
        out_shape=[jax.ShapeDtypeStruct((seq, N_DIFF_HEADS * DIFF_V_DIM), BF16)]
        + [jax.ShapeDtypeStruct(w.shape, BF16) for w in f32_weights],
        scratch_shapes=[pltpu.VMEM((seq // tkc, tkc, 2 * tq), BF16),
                        pltpu.VMEM((8, 2 * tq), F32),
                        pltpu.VMEM((8, 2 * tq), F32),
                        pltpu.VMEM((DIFF_V_DIM, tq), F32)],
        compiler_params=_params(("arbitrary", "arbitrary")),
        name="diff_attention",
    )(lq1, lk1, lq2, lk2, q, k, v_t, subln_g, norm_sq_max, *f32_weights)
    return outs[0], outs[1:]


def _outproj_kernel(a_ref, gm_ref, w_ref, x_ref, gpost_ref, gffn_ref, x1_ref, h2_ref, *, row_block):
    tm, half = a_ref.shape
    for r in range(0, tm, row_block):
        rows = slice(r, r + row_block)
        mix = (jnp.dot(a_ref[rows, :], w_ref[0:half, :], preferred_element_type=F32)
               + jnp.dot(gm_ref[rows, :], w_ref[half:2 * half, :], preferred_element_type=F32))
        x1 = x_ref[rows, :] + _rms(mix, RMS_EPS) * gpost_ref[...]
        x1_ref[rows, :] = x1
        h2_ref[rows, :] = (_rms(x1, RMS_EPS) * gffn_ref[...]).astype(BF16)


def _outproj(attn, gm, w_out, x, post_mix_g, pre_ffn_g):
    seq, d_model = x.shape
    half = attn.shape[1]
    tm = 512
    row_spec = lambda cols: pl.BlockSpec((tm, cols), lambda i: (i, 0))
    const_spec = lambda shape: pl.BlockSpec(shape, lambda i: (0, 0))
    return pl.pallas_call(
        functools.partial(_outproj_kernel, row_block=128),
        grid=(seq // tm,),
        in_specs=[row_spec(half), row_spec(half), const_spec(w_out.shape), row_spec(d_model),
                  const_spec((1, d_model)), const_spec((1, d_model))],
        out_specs=[row_spec(d_model), row_spec(d_model)],
        out_shape=[jax.ShapeDtypeStruct((seq, d_model), F32), jax.ShapeDtypeStruct((seq, d_model), BF16)],
        compiler_params=_params(("arbitrary",)),
        name="outproj",
    )(attn, gm, w_out, x, post_mix_g, pre_ffn_g)


def _ffn_kernel(h_ref, wg_ref, wu_ref, wd_ref, x1_hbm, g_ref, o_ref, x1_buf, x1_sem):
    i = pl.program_id(0)
    j = pl.program_id(1)
    tm = o_ref.shape[0]
    x1_copy = pltpu.make_async_copy(x1_hbm.at[pl.ds(pl.multiple_of(i * tm, tm), tm), :], x1_buf, x1_sem)

    def chunk_output(rows):
        h = h_ref[rows, :]
        gate = jnp.dot(h, wg_ref[...], preferred_element_type=F32)
        up = jnp.dot(h, wu_ref[...], preferred_element_type=F32)
        act = (gate * jax.nn.sigmoid(gate) * up).astype(BF16)
        return jnp.dot(act, wd_ref[...], preferred_element_type=F32)

    def accumulate(rows):
        o_ref[rows, :] += chunk_output(rows)

    last = pl.num_programs(1) - 1

    @pl.when(j == 0)
    def _():
        x1_copy.start()
        o_ref[...] = chunk_output(slice(0, tm))

    @pl.when(jnp.logical_and(j > 0, j < last))
    def _():
        accumulate(slice(0, tm))

    @pl.when(j == last)
    def _():
        x1_copy.wait()
        for r in range(0, tm, tm // 2):
            rows = slice(r, r + tm // 2)
            accumulate(rows)
            o_ref[rows, :] = x1_buf[rows, :] + _rms(o_ref[rows, :], RMS_EPS) * g_ref[...]


def _ffn(h2, w_gate, w_up, w_down, x1, post_ffn_g):
    seq, d_model = x1.shape
    d_ff = w_gate.shape[1]
    tm = 1024
    tf = 512
    return pl.pallas_call(
        _ffn_kernel,
        grid=(seq // tm, d_ff // tf),
        in_specs=[
            pl.BlockSpec((tm, d_model), lambda i, j: (i, 0)),
            pl.BlockSpec((d_model, tf), lambda i, j: (0, j)),
            pl.BlockSpec((d_model, tf), lambda i, j: (0, j)),
            pl.BlockSpec((tf, d_model), lambda i, j: (j, 0)),
            pl.BlockSpec(memory_space=pl.ANY),
            pl.BlockSpec((1, d_model), lambda i, j: (0, 0)),
        ],
        out_specs=pl.BlockSpec((tm, d_model), lambda i, j: (i, 0)),
        out_shape=jax.ShapeDtypeStruct((seq, d_model), F32),
        scratch_shapes=[pltpu.VMEM((tm, d_model), F32), pltpu.SemaphoreType.DMA(())],
        compiler_params=_params(("arbitrary", "arbitrary")),
        name="ffn",
    )(h2, w_gate, w_up, w_down, x1, post_ffn_g)


def kernel(x, positions, pre_mix_g, w_in, lambda_q1, lambda_k1, lambda_q2, lambda_k2, subln_g,
           gmlp_ln_g, gmlp_ln_b, w_s, b_s, w_out, post_mix_g, pre_ffn_g, w_gate, w_up, w_down,
           post_ffn_g):
    batch, seq, d_model = x.shape
    assert batch == 1 and pre_mix_g.shape[0] == 1, "single sequence, single layer"
    assert seq % 1024 == 0 and d_model == 2048
    x2 = x[0]
    inv_freq = ROPE_THETA ** (-jnp.arange(0, DIFF_HEAD_DIM, 2, dtype=F32) / DIFF_HEAD_DIM)
    inv_lanes = jnp.tile(inv_freq, LANES // (DIFF_HEAD_DIM // 2))[None, :]
    bs_full = jnp.repeat(b_s[0].T, LANES, axis=1)
    q, k, v_t, gm, norm_sq_max = _inproj(x2, pre_mix_g, w_in[0].astype(BF16), positions[0][:, None], inv_lanes,
                          gmlp_ln_g, gmlp_ln_b, w_s[0].astype(BF16), bs_full)
    attn, (w_out16, w_gate16, w_up16, w_down16) = _attention(
        lambda_q1, lambda_k1, lambda_q2, lambda_k2, q, k, v_t, subln_g, norm_sq_max,
        (w_out[0], w_gate[0], w_up[0], w_down[0]))
    x1, h2 = _outproj(attn, gm, w_out16, x2, post_mix_g, pre_ffn_g)
    out = _ffn(h2, w_gate16, w_up16, w_down16, x1, post_ffn_g)
    return out[None]
```

```python
import functools
import math

import jax
import jax.numpy as jnp
from jax import lax
from jax.experimental import pallas as pl
from jax.experimental.pallas import tpu as pltpu

N_DIFF_HEADS = 8
DIFF_HEAD_DIM = 64
DIFF_V_DIM = 2 * DIFF_HEAD_DIM
N_GMLP_GROUPS = 8
CHUNK = 128
ROPE_THETA = 10000.0
RMS_EPS = 1e-6
LN_EPS = 1e-5
SUBLN_EPS = 1e-5
LAMBDA_INIT = 0.8 - 0.6 * math.exp(-0.3 * 0)

LANES = 128
F32_SUBLANES = 8
BF16_SUBLANES = 16
VMEM_LIMIT_BYTES = 60 * 1024 * 1024

F32 = jnp.float32
BF16 = jnp.bfloat16


def _rms(x, eps):
    return x * lax.rsqrt(jnp.mean(x * x, axis=-1, keepdims=True) + eps)


def _gelu(x):
    return 0.5 * x * (1.0 + lax.erf(x * (1.0 / math.sqrt(2.0))))


def _params(semantics):
    return pltpu.CompilerParams(dimension_semantics=semantics, vmem_limit_bytes=VMEM_LIMIT_BYTES)


def _first_half_lanes(shape):
    lane = lax.broadcasted_iota(jnp.int32, shape, 1)
    return (lane % DIFF_HEAD_DIM) < (DIFF_HEAD_DIM // 2)


def _rope(x, cos, sin_signed):
    half = DIFF_HEAD_DIM // 2
    partner = jnp.where(_first_half_lanes(x.shape), pltpu.roll(x, LANES - half, 1), pltpu.roll(x, half, 1))
    return x * cos + partner * sin_signed


def _inproj_kernel(x_ref, g_ref, w_ref, pos_ref, inv_ref, lng_ref, lnb_ref, ws_ref, bs_ref,
                   q_ref, k_ref, vt_ref, gm_ref, nmax_ref, h_ref, gu_ref, *, q_scale):
    tm, width = q_ref.shape
    q_cols, k_cols, v_cols, gu_cols, gv_cols = (slice(n * width, (n + 1) * width) for n in range(5))
    def proj(cols):
        return jnp.dot(h_ref[...], w_ref[:, cols], preferred_element_type=F32)

    h_ref[...] = (_rms(x_ref[...], RMS_EPS) * g_ref[...]).astype(BF16)
    vt_ref[...] = proj(v_cols).T.astype(BF16)

    ang = pos_ref[...].astype(F32) * inv_ref[...]
    sin = jnp.sin(ang)
    sin = jnp.where(_first_half_lanes(ang.shape), -sin, sin)
    cos = jnp.cos(ang)

    map1_lanes = lax.broadcasted_iota(jnp.int32, (tm, LANES), 1) < DIFF_HEAD_DIM
    nmax_lane = lax.broadcasted_iota(jnp.int32, nmax_ref.shape[1:], 1)

    def rope_store(out_ref, cols, scale, nmax, first_lane):
        acc = proj(cols)
        for c in range(width // LANES):
            sl = slice(c * LANES, (c + 1) * LANES)
            rotated = (_rope(acc[:, sl], cos, sin) * scale).astype(BF16)
            out_ref[:, sl] = rotated
            sq = rotated.astype(F32)
            sq = sq * sq
            for m, in_map in enumerate((map1_lanes, jnp.logical_not(map1_lanes))):
                norm_sq = jnp.sum(jnp.where(in_map, sq, 0.0), axis=1, keepdims=True)
                nmax = jnp.where(nmax_lane == first_lane + 2 * c + m,
                                 jnp.max(norm_sq, axis=0, keepdims=True), nmax)
        return nmax

    nmax = rope_store(q_ref, q_cols, q_scale, jnp.zeros(nmax_ref.shape[1:], F32), 0)
    nmax_ref[0] = rope_store(k_ref, k_cols, 1.0, nmax, 2 * N_DIFF_HEADS)

    gu_ref[...] = _gelu(proj(gu_cols))
    gv = _gelu(proj(gv_cols))
    mu = jnp.mean(gv, axis=-1, keepdims=True)
    d = gv - mu
    var = jnp.mean(d * d, axis=-1, keepdims=True)
    vln = (d * lax.rsqrt(var + LN_EPS) * lng_ref[...] + lnb_ref[...]).astype(BF16)
    for c in range(tm // CHUNK):
        rows = slice(c * CHUNK, (c + 1) * CHUNK)
        for g in range(N_GMLP_GROUPS):
            cols = slice(g * LANES, (g + 1) * LANES)
            y = jnp.dot(ws_ref[g], vln[rows, cols], preferred_element_type=F32) + bs_ref[:, cols]
            gm_ref[rows, cols] = (gu_ref[rows, cols] * y).astype(BF16)


def _inproj(x, pre_g, w_in, positions, inv_lanes, ln_g, ln_b, w_s, bs_full):
    seq, d_model = x.shape
    width = w_in.shape[1] // 5
    tm = 512
    row_spec = lambda cols: pl.BlockSpec((tm, cols), lambda i: (i, 0))
    const_spec = lambda shape, **kw: pl.BlockSpec(shape, lambda i: (0,) * len(shape), **kw)
    return pl.pallas_call(
        functools.partial(_inproj_kernel, q_scale=DIFF_HEAD_DIM ** -0.5 * math.log2(math.e)),
        grid=(seq // tm,),
        in_specs=[
            row_spec(d_model),
            const_spec((1, d_model)),
            const_spec(w_in.shape, pipeline_mode=pl.Buffered(1)),
            row_spec(1),
            const_spec((1, LANES)),
            const_spec((1, width)),
            const_spec((1, width)),
            const_spec(w_s.shape),
            const_spec(bs_full.shape),
        ],
        out_specs=[row_spec(width), row_spec(width), pl.BlockSpec((width, tm), lambda i: (0, i)),
                   row_spec(width), pl.BlockSpec((1, F32_SUBLANES, LANES), lambda i: (i, 0, 0))],
        out_shape=[jax.ShapeDtypeStruct((seq, width), BF16), jax.ShapeDtypeStruct((seq, width), BF16),
                   jax.ShapeDtypeStruct((width, seq), BF16), jax.ShapeDtypeStruct((seq, width), BF16),
                   jax.ShapeDtypeStruct((seq // tm, F32_SUBLANES, LANES), F32)],
        scratch_shapes=[pltpu.VMEM((tm, d_model), BF16), pltpu.VMEM((tm, width), F32)],
        compiler_params=_params(("arbitrary",)),
        name="inproj",
    )(x, pre_g, w_in, positions, inv_lanes, ln_g, ln_b, w_s, bs_full)


def _attn_kernel(*refs, tkc, n_cast):
    lq1_ref, lk1_ref, lq2_ref, lk2_ref, q_ref, k_ref, vt_ref, g_ref, nmax_ref = refs[:9]
    cast_src = refs[9:9 + n_cast]
    o_ref = refs[9 + n_cast]
    cast_dst = refs[10 + n_cast:10 + 2 * n_cast]
    p_ref, shift_ref, lpart_ref, acc_ref = refs[10 + 2 * n_cast:]

    for src, dst in zip(cast_src, cast_dst):
        dst[...] = src[...].astype(BF16)

    tq = q_ref.shape[0]
    two_tq = 2 * tq
    nkc = k_ref.shape[0] // tkc
    lane = lax.broadcasted_iota(jnp.int32, (tq, LANES), 1)
    in_map1 = lane < DIFF_HEAD_DIM
    q = q_ref[...]
    zero = jnp.zeros_like(q)
    q_maps = jnp.concatenate([jnp.where(in_map1, q, zero), jnp.where(in_map1, zero, q)], axis=0)
    nt_dims = (((1,), (1,)), ((), ()))
    col_tiles = [slice(c, c + LANES) for c in range(0, two_tq, LANES)]

    def key_rows(kc):
        if isinstance(kc, int):
            return slice(kc * tkc, (kc + 1) * tkc)
        return pl.ds(pl.multiple_of(kc * tkc, tkc), tkc)

    def score_tiles(kc):
        s = lax.dot_general(k_ref[key_rows(kc), :], q_maps, nt_dims, preferred_element_type=F32)
        return [s[:, cols].reshape(tkc // F32_SUBLANES, F32_SUBLANES, LANES) for cols in col_tiles]

    def shifted_probs(kc):
        for cols, tile in zip(col_tiles, score_tiles(kc)):
            p = jnp.exp2(tile - shift_ref[:, cols])
            p_ref[kc, :, cols] = p.reshape(tkc, LANES).astype(BF16)
            lpart_ref[:, cols] += jnp.sum(p, axis=0)

    def column_max(kc):
        for cols, tile in zip(col_tiles, score_tiles(kc)):
            lpart_ref[:, cols] = jnp.maximum(lpart_ref[:, cols], jnp.max(tile, axis=0))

    def sublane_reduce(op):
        return op(lpart_ref[...], axis=0, keepdims=True)

    norm_sq_max = jnp.max(nmax_ref[...], axis=0)
    nmax_lane = lax.broadcasted_iota(jnp.int32, norm_sq_max.shape, 1)
    head_lane = 2 * pl.program_id(0)

    def norm_sq(lane_index):
        picked = jnp.where(nmax_lane == lane_index, norm_sq_max, 0.0)
        return jnp.max(jnp.max(picked, axis=1, keepdims=True), axis=0, keepdims=True)

    for mp in range(2):
        bound = jnp.sqrt(norm_sq(head_lane + mp) * norm_sq(2 * N_DIFF_HEADS + head_lane + mp))
        shift_ref[:, mp * tq:(mp + 1) * tq] = jnp.broadcast_to(bound, (F32_SUBLANES, tq))
    lpart_ref[...] = jnp.zeros_like(lpart_ref)
    for kc in range(nkc):
        shifted_probs(kc)
    bounded_sums = sublane_reduce(jnp.sum)

    def redo_with_column_max():
        lpart_ref[...] = jnp.full(lpart_ref.shape, -jnp.inf, F32)
        lax.fori_loop(0, nkc, lambda kc, c: (column_max(kc), c)[1], 0)
        shift_ref[...] = jnp.broadcast_to(sublane_reduce(jnp.max), shift_ref.shape)
        lpart_ref[...] = jnp.zeros_like(lpart_ref)
        lax.fori_loop(0, nkc, lambda kc, c: (shifted_probs(kc), c)[1], 0)
        return sublane_reduce(jnp.sum)

    sums = lax.cond(jnp.min(bounded_sums) >= 2.0 ** -60, lambda: bounded_sums, redo_with_column_max)
    l1 = sums[:, :tq]
    l2 = sums[:, tq:]

    lam = (jnp.exp(jnp.sum(lq1_ref[...] * lk1_ref[...], axis=1, keepdims=True))
           - jnp.exp(jnp.sum(lq2_ref[...] * lk2_ref[...], axis=1, keepdims=True)) + LAMBDA_INIT)
    ratio = jnp.broadcast_to(lam * l1 / l2, (BF16_SUBLANES, tq)).astype(BF16)
    acc_ref[...] = jnp.zeros_like(acc_ref)
    for kc in range(nkc):
        p1 = p_ref[kc, :, 0:tq].reshape(tkc // BF16_SUBLANES, BF16_SUBLANES, tq)
        p2 = p_ref[kc, :, tq:two_tq].reshape(tkc // BF16_SUBLANES, BF16_SUBLANES, tq)
        w = (p1 - p2 * ratio).reshape(tkc, tq)
        acc_ref[...] += jnp.dot(vt_ref[:, kc * tkc:(kc + 1) * tkc], w, preferred_element_type=F32)

    out_t = acc_ref[...] * (1.0 / l1)
    out_t = out_t * lax.rsqrt(jnp.mean(out_t * out_t, axis=0, keepdims=True) + SUBLN_EPS)
    o_ref[...] = (out_t.T * g_ref[...] * (1.0 - LAMBDA_INIT)).astype(BF16)


def _cast_block_rows(rows, n_steps):
    for block in range(BF16_SUBLANES, rows + 1, BF16_SUBLANES):
        if rows % block == 0 and rows // block <= n_steps:
            return block
    raise ValueError(f"no row block for {rows} rows in {n_steps} steps")


def _attention(lq1, lk1, lq2, lk2, q, k, v_t, subln_g, norm_sq_max, f32_weights):
    seq = q.shape[0]
    tq = 1024
    tkc = 512
    n_q = seq // tq
    n_steps = N_DIFF_HEADS * n_q
    lam_spec = pl.BlockSpec((1, DIFF_HEAD_DIM), lambda h, i: (0, 0))
    cast_specs = []
    for w in f32_weights:
        block = _cast_block_rows(w.shape[0], n_steps)
        n_blocks = w.shape[0] // block
        cast_specs.append(pl.BlockSpec(
            (block, w.shape[1]), lambda h, i, last=n_blocks - 1: (jnp.minimum(h * n_q + i, last), 0)))
    outs = pl.pallas_call(
        functools.partial(_attn_kernel, tkc=tkc, n_cast=len(f32_weights)),
        grid=(N_DIFF_HEADS, n_q),
        in_specs=[
            lam_spec, lam_spec, lam_spec, lam_spec,
            pl.BlockSpec((tq, DIFF_V_DIM), lambda h, i: (i, h)),
            pl.BlockSpec((seq, DIFF_V_DIM), lambda h, i: (0, h)),
            pl.BlockSpec((DIFF_V_DIM, seq), lambda h, i: (h, 0)),
            pl.BlockSpec((1, DIFF_V_DIM), lambda h, i: (0, 0)),
            pl.BlockSpec(norm_sq_max.shape, lambda h, i: (0, 0, 0)),
        ] + cast_specs,
        out_specs=[pl.BlockSpec((tq, DIFF_V_DIM), lambda h, i: (i, h))] + cast_specs,
        out_shape=[jax.ShapeDtypeStruct((seq, N_DIFF_HEADS * DIFF_V_DIM), BF16)]
        + [jax.ShapeDtypeStruct(w.shape, BF16) for w in f32_weights],
        scratch_shapes=[pltpu.VMEM((seq // tkc, tkc, 2 * tq), BF16),
                        pltpu.VMEM((F32_SUBLANES, 2 * tq), F32),
                        pltpu.VMEM((F32_SUBLANES, 2 * tq), F32),
                        pltpu.VMEM((DIFF_V_DIM, tq), F32)],
        compiler_params=_params(("arbitrary", "arbitrary")),
        name="diff_attention",
    )(lq1, lk1, lq2, lk2, q, k, v_t, subln_g, norm_sq_max, *f32_weights)
    return outs[0], outs[1:]


def _outproj_kernel(a_ref, gm_ref, w_ref, x_ref, gpost_ref, gffn_ref, x1_ref, h2_ref, *, row_block):
    tm, half = a_ref.shape
    for r in range(0, tm, row_block):
        rows = slice(r, r + row_block)
        mix = (jnp.dot(a_ref[rows, :], w_ref[0:half, :], preferred_element_type=F32)
               + jnp.dot(gm_ref[rows, :], w_ref[half:2 * half, :], preferred_element_type=F32))
        x1 = x_ref[rows, :] + _rms(mix, RMS_EPS) * gpost_ref[...]
        x1_ref[rows, :] = x1
        h2_ref[rows, :] = (_rms(x1, RMS_EPS) * gffn_ref[...]).astype(BF16)


def _outproj(attn, gm, w_out, x, post_mix_g, pre_ffn_g):
    seq, d_model = x.shape
    half = attn.shape[1]
    tm = 512
    row_spec = lambda cols: pl.BlockSpec((tm, cols), lambda i: (i, 0))
    const_spec = lambda shape: pl.BlockSpec(shape, lambda i: (0, 0))
    return pl.pallas_call(
        functools.partial(_outproj_kernel, row_block=128),
        grid=(seq // tm,),
        in_specs=[row_spec(half), row_spec(half), const_spec(w_out.shape), row_spec(d_model),
                  const_spec((1, d_model)), const_spec((1, d_model))],
        out_specs=[row_spec(d_model), row_spec(d_model)],
        out_shape=[jax.ShapeDtypeStruct((seq, d_model), F32), jax.ShapeDtypeStruct((seq, d_model), BF16)],
        compiler_params=_params(("arbitrary",)),
        name="outproj",
    )(attn, gm, w_out, x, post_mix_g, pre_ffn_g)


def _ffn_kernel(h_ref, wg_ref, wu_ref, wd_ref, x1_hbm, g_ref, o_ref, x1_buf, x1_sem):
    i = pl.program_id(0)
    j = pl.program_id(1)
    tm = o_ref.shape[0]
    x1_copy = pltpu.make_async_copy(x1_hbm.at[pl.ds(pl.multiple_of(i * tm, tm), tm), :], x1_buf, x1_sem)

    @pl.when(j == 0)
    def _():
        x1_copy.start()
        o_ref[...] = jnp.zeros_like(o_ref)

    def accumulate(rows):
        h = h_ref[rows, :]
        gate = jnp.dot(h, wg_ref[...], preferred_element_type=F32)
        up = jnp.dot(h, wu_ref[...], preferred_element_type=F32)
        act = (gate * jax.nn.sigmoid(gate) * up).astype(BF16)
        o_ref[rows, :] += jnp.dot(act, wd_ref[...], preferred_element_type=F32)

    last = pl.num_programs(1) - 1

    @pl.when(j < last)
    def _():
        accumulate(slice(0, tm))

    @pl.when(j == last)
    def _():
        x1_copy.wait()
        for r in range(0, tm, tm // 2):
            rows = slice(r, r + tm // 2)
            accumulate(rows)
            o_ref[rows, :] = x1_buf[rows, :] + _rms(o_ref[rows, :], RMS_EPS) * g_ref[...]


def _ffn(h2, w_gate, w_up, w_down, x1, post_ffn_g):
    seq, d_model = x1.shape
    d_ff = w_gate.shape[1]
    tm = 1024
    tf = 512
    return pl.pallas_call(
        _ffn_kernel,
        grid=(seq // tm, d_ff // tf),
        in_specs=[
            pl.BlockSpec((tm, d_model), lambda i, j: (i, 0)),
            pl.BlockSpec((d_model, tf), lambda i, j: (0, j)),
            pl.BlockSpec((d_model, tf), lambda i, j: (0, j)),
            pl.BlockSpec((tf, d_model), lambda i, j: (j, 0)),
            pl.BlockSpec(memory_space=pl.ANY),
            pl.BlockSpec((1, d_model), lambda i, j: (0, 0)),
        ],
        out_specs=pl.BlockSpec((tm, d_model), lambda i, j: (i, 0)),
        out_shape=jax.ShapeDtypeStruct((seq, d_model), F32),
        scratch_shapes=[pltpu.VMEM((tm, d_model), F32), pltpu.SemaphoreType.DMA(())],
        compiler_params=_params(("arbitrary", "arbitrary")),
        name="ffn",
    )(h2, w_gate, w_up, w_down, x1, post_ffn_g)


def kernel(x, positions, pre_mix_g, w_in, lambda_q1, lambda_k1, lambda_q2, lambda_k2, subln_g,
           gmlp_ln_g, gmlp_ln_b, w_s, b_s, w_out, post_mix_g, pre_ffn_g, w_gate, w_up, w_down,
           post_ffn_g):
    batch, seq, d_model = x.shape
    assert batch == 1 and pre_mix_g.shape[0] == 1, "single sequence, single layer"
    assert seq % 1024 == 0 and d_model == 2048
    x2 = x[0]
    inv_freq = ROPE_THETA ** (-jnp.arange(0, DIFF_HEAD_DIM, 2, dtype=F32) / DIFF_HEAD_DIM)
    inv_lanes = jnp.tile(inv_freq, LANES // (DIFF_HEAD_DIM // 2))[None, :]
    bs_full = jnp.repeat(b_s[0].T, LANES, axis=1)
    q, k, v_t, gm, norm_sq_max = _inproj(x2, pre_mix_g, w_in[0].astype(BF16), positions[0][:, None], inv_lanes,
                          gmlp_ln_g, gmlp_ln_b, w_s[0].astype(BF16), bs_full)
    attn, (w_out16, w_gate16, w_up16, w_down16) = _attention(
        lambda_q1, lambda_k1, lambda_q2, lambda_k2, q, k, v_t, subln_g, norm_sq_max,
        (w_out[0], w_gate[0], w_up[0], w_down[0]))
    x1, h2 = _outproj(attn, gm, w_out16, x2, post_mix_g, pre_ffn_g)
    out = _ffn(h2, w_gate16, w_up16, w_down16, x1, post_ffn_g)
    return out[None]
```

```python
import functools
import math

import jax
import jax.numpy as jnp
from jax import lax
from jax.experimental import pallas as pl
from jax.experimental.pallas import tpu as pltpu

N_DIFF_HEADS = 8
DIFF_HEAD_DIM = 64
DIFF_V_DIM = 2 * DIFF_HEAD_DIM
N_GMLP_GROUPS = 8
CHUNK = 128
ROPE_THETA = 10000.0
RMS_EPS = 1e-6
LN_EPS = 1e-5
SUBLN_EPS = 1e-5
LAMBDA_INIT = 0.8 - 0.6 * math.exp(-0.3 * 0)

LANES = 128
F32_SUBLANES = 8
BF16_SUBLANES = 16
VMEM_LIMIT_BYTES = 60 * 1024 * 1024

F32 = jnp.float32
BF16 = jnp.bfloat16


def _rms(x, eps):
    return x * lax.rsqrt(jnp.mean(x * x, axis=-1, keepdims=True) + eps)


def _gelu(x):
    return 0.5 * x * (1.0 + lax.erf(x * (1.0 / math.sqrt(2.0))))


def _params(semantics):
    return pltpu.CompilerParams(dimension_semantics=semantics, vmem_limit_bytes=VMEM_LIMIT_BYTES)


def _first_half_lanes(shape):
    lane = lax.broadcasted_iota(jnp.int32, shape, 1)
    return (lane % DIFF_HEAD_DIM) < (DIFF_HEAD_DIM // 2)


def _rope(x, cos, sin_signed):
    half = DIFF_HEAD_DIM // 2
    partner = jnp.where(_first_half_lanes(x.shape), pltpu.roll(x, LANES - half, 1), pltpu.roll(x, half, 1))
    return x * cos + partner * sin_signed


def _inproj_kernel(x_ref, g_ref, w_ref, pos_ref, inv_ref, lng_ref, lnb_ref, ws_ref, bs_ref,
                   q_ref, k_ref, vt_ref, gm_ref, nmax_ref, h_ref, gu_ref, *, q_scale):
    tm, width = q_ref.shape
    q_cols, k_cols, v_cols, gu_cols, gv_cols = (slice(n * width, (n + 1) * width) for n in range(5))
    def proj(cols):
        return jnp.dot(h_ref[...], w_ref[:, cols], preferred_element_type=F32)

    h_ref[...] = (_rms(x_ref[...], RMS_EPS) * g_ref[...]).astype(BF16)
    vt_ref[...] = proj(v_cols).T.astype(BF16)

    ang = pos_ref[...].astype(F32) * inv_ref[...]
    sin = jnp.sin(ang)
    sin = jnp.where(_first_half_lanes(ang.shape), -sin, sin)
    cos = jnp.cos(ang)

    map1_lanes = lax.broadcasted_iota(jnp.int32, (tm, LANES), 1) < DIFF_HEAD_DIM
    nmax_lane = lax.broadcasted_iota(jnp.int32, nmax_ref.shape[1:], 1)

    def rope_store(out_ref, cols, scale, nmax, first_lane):
        acc = proj(cols)
        for c in range(width // LANES):
            sl = slice(c * LANES, (c + 1) * LANES)
            rotated = (_rope(acc[:, sl], cos, sin) * scale).astype(BF16)
            out_ref[:, sl] = rotated
            sq = rotated.astype(F32)
            sq = sq * sq
            for m, in_map in enumerate((map1_lanes, jnp.logical_not(map1_lanes))):
                norm_sq = jnp.sum(jnp.where(in_map, sq, 0.0), axis=1, keepdims=True)
                nmax = jnp.where(nmax_lane == first_lane + 2 * c + m,
                                 jnp.max(norm_sq, axis=0, keepdims=True), nmax)
        return nmax

    nmax = rope_store(q_ref, q_cols, q_scale, jnp.zeros(nmax_ref.shape[1:], F32), 0)
    nmax_ref[0] = rope_store(k_ref, k_cols, 1.0, nmax, 2 * N_DIFF_HEADS)

    gu_ref[...] = _gelu(proj(gu_cols))
    gv = _gelu(proj(gv_cols))
    mu = jnp.mean(gv, axis=-1, keepdims=True)
    d = gv - mu
    var = jnp.mean(d * d, axis=-1, keepdims=True)
    vln = (d * lax.rsqrt(var + LN_EPS) * lng_ref[...] + lnb_ref[...]).astype(BF16)
    for c in range(tm // CHUNK):
        rows = slice(c * CHUNK, (c + 1) * CHUNK)
        for g in range(N_GMLP_GROUPS):
            cols = slice(g * LANES, (g + 1) * LANES)
            y = jnp.dot(ws_ref[g], vln[rows, cols], preferred_element_type=F32) + bs_ref[:, cols]
            gm_ref[rows, cols] = (gu_ref[rows, cols] * y).astype(BF16)


def _inproj(x, pre_g, w_in, positions, inv_lanes, ln_g, ln_b, w_s, bs_full):
    seq, d_model = x.shape
    width = w_in.shape[1] // 5
    tm = 512
    row_spec = lambda cols: pl.BlockSpec((tm, cols), lambda i: (i, 0))
    const_spec = lambda shape, **kw: pl.BlockSpec(shape, lambda i: (0,) * len(shape), **kw)
    return pl.pallas_call(
        functools.partial(_inproj_kernel, q_scale=DIFF_HEAD_DIM ** -0.5 * math.log2(math.e)),
        grid=(seq // tm,),
        in_specs=[
            row_spec(d_model),
            const_spec((1, d_model)),
            const_spec(w_in.shape, pipeline_mode=pl.Buffered(1)),
            row_spec(1),
            const_spec((1, LANES)),
            const_spec((1, width)),
            const_spec((1, width)),
            const_spec(w_s.shape),
            const_spec(bs_full.shape),
        ],
        out_specs=[row_spec(width), row_spec(width), pl.BlockSpec((width, tm), lambda i: (0, i)),
                   row_spec(width), pl.BlockSpec((1, F32_SUBLANES, LANES), lambda i: (i, 0, 0))],
        out_shape=[jax.ShapeDtypeStruct((seq, width), BF16), jax.ShapeDtypeStruct((seq, width), BF16),
                   jax.ShapeDtypeStruct((width, seq), BF16), jax.ShapeDtypeStruct((seq, width), BF16),
                   jax.ShapeDtypeStruct((seq // tm, F32_SUBLANES, LANES), F32)],
        scratch_shapes=[pltpu.VMEM((tm, d_model), BF16), pltpu.VMEM((tm, width), F32)],
        compiler_params=_params(("arbitrary",)),
        name="inproj",
    )(x, pre_g, w_in, positions, inv_lanes, ln_g, ln_b, w_s, bs_full)


def _attn_kernel(*refs, tkc, n_cast):
    lq1_ref, lk1_ref, lq2_ref, lk2_ref, q_ref, k_ref, vt_ref, g_ref, nmax_ref = refs[:9]
    cast_src = refs[9:9 + n_cast]
    o_ref = refs[9 + n_cast]
    cast_dst = refs[10 + n_cast:10 + 2 * n_cast]
    p_ref, shift_ref, lpart_ref, acc_ref = refs[10 + 2 * n_cast:]

    tq = q_ref.shape[0]
    two_tq = 2 * tq
    nkc = k_ref.shape[0] // tkc
    lane = lax.broadcasted_iota(jnp.int32, (tq, LANES), 1)
    in_map1 = lane < DIFF_HEAD_DIM
    q = q_ref[...]
    zero = jnp.zeros_like(q)
    q_maps = jnp.concatenate([jnp.where(in_map1, q, zero), jnp.where(in_map1, zero, q)], axis=0)
    nt_dims = (((1,), (1,)), ((), ()))
    col_tiles = [slice(c, c + LANES) for c in range(0, two_tq, LANES)]

    def key_rows(kc):
        if isinstance(kc, int):
            return slice(kc * tkc, (kc + 1) * tkc)
        return pl.ds(pl.multiple_of(kc * tkc, tkc), tkc)

    def score_tiles(kc):
        s = lax.dot_general(k_ref[key_rows(kc), :], q_maps, nt_dims, preferred_element_type=F32)
        return [s[:, cols].reshape(tkc // F32_SUBLANES, F32_SUBLANES, LANES) for cols in col_tiles]

    def shifted_probs(kc):
        for cols, tile in zip(col_tiles, score_tiles(kc)):
            p = jnp.exp2(tile - shift_ref[:, cols])
            p_ref[kc, :, cols] = p.reshape(tkc, LANES).astype(BF16)
            lpart_ref[:, cols] += jnp.sum(p, axis=0)

    def column_max(kc):
        for cols, tile in zip(col_tiles, score_tiles(kc)):
            lpart_ref[:, cols] = jnp.maximum(lpart_ref[:, cols], jnp.max(tile, axis=0))

    def sublane_reduce(op):
        return op(lpart_ref[...], axis=0, keepdims=True)

    norm_sq_max = jnp.max(nmax_ref[...], axis=0)
    nmax_lane = lax.broadcasted_iota(jnp.int32, norm_sq_max.shape, 1)
    head_lane = 2 * pl.program_id(0)

    def norm_sq(lane_index):
        picked = jnp.where(nmax_lane == lane_index, norm_sq_max, 0.0)
        return jnp.max(jnp.max(picked, axis=1, keepdims=True), axis=0, keepdims=True)

    for mp in range(2):
        bound = jnp.sqrt(norm_sq(head_lane + mp) * norm_sq(2 * N_DIFF_HEADS + head_lane + mp))
        shift_ref[:, mp * tq:(mp + 1) * tq] = jnp.broadcast_to(bound, (F32_SUBLANES, tq))
    lpart_ref[...] = jnp.zeros_like(lpart_ref)
    for kc in range(nkc):
        shifted_probs(kc)
    bounded_sums = sublane_reduce(jnp.sum)

    def redo_with_column_max():
        lpart_ref[...] = jnp.full(lpart_ref.shape, -jnp.inf, F32)
        lax.fori_loop(0, nkc, lambda kc, c: (column_max(kc), c)[1], 0)
        shift_ref[...] = jnp.broadcast_to(sublane_reduce(jnp.max), shift_ref.shape)
        lpart_ref[...] = jnp.zeros_like(lpart_ref)
        lax.fori_loop(0, nkc, lambda kc, c: (shifted_probs(kc), c)[1], 0)
        return sublane_reduce(jnp.sum)

    sums = lax.cond(jnp.min(bounded_sums) >= 2.0 ** -60, lambda: bounded_sums, redo_with_column_max)
    l1 = sums[:, :tq]
    l2 = sums[:, tq:]

    for src, dst in zip(cast_src, cast_dst):
        dst[...] = src[...].astype(BF16)

    lam = (jnp.exp(jnp.sum(lq1_ref[...] * lk1_ref[...], axis=1, keepdims=True))
           - jnp.exp(jnp.sum(lq2_ref[...] * lk2_ref[...], axis=1, keepdims=True)) + LAMBDA_INIT)
    ratio = jnp.broadcast_to(lam * l1 / l2, (BF16_SUBLANES, tq)).astype(BF16)
    acc_ref[...] = jnp.zeros_like(acc_ref)
    for kc in range(nkc):
        p1 = p_ref[kc, :, 0:tq].reshape(tkc // BF16_SUBLANES, BF16_SUBLANES, tq)
        p2 = p_ref[kc, :, tq:two_tq].reshape(tkc // BF16_SUBLANES, BF16_SUBLANES, tq)
        w = (p1 - p2 * ratio).reshape(tkc, tq)
        acc_ref[...] += jnp.dot(vt_ref[:, kc * tkc:(kc + 1) * tkc], w, preferred_element_type=F32)

    out_t = acc_ref[...] * (1.0 / l1)
    out_t = out_t * lax.rsqrt(jnp.mean(out_t * out_t, axis=0, keepdims=True) + SUBLN_EPS)
    o_ref[...] = (out_t.T * g_ref[...] * (1.0 - LAMBDA_INIT)).astype(BF16)


def _cast_block_rows(rows, n_steps):
    for block in range(BF16_SUBLANES, rows + 1, BF16_SUBLANES):
        if rows % block == 0 and rows // block <= n_steps:
            return block
    raise ValueError(f"no row block for {rows} rows in {n_steps} steps")


def _attention(lq1, lk1, lq2, lk2, q, k, v_t, subln_g, norm_sq_max, f32_weights):
    seq = q.shape[0]
    tq = 1024
    tkc = 512
    n_q = seq // tq
    n_steps = N_DIFF_HEADS * n_q
    lam_spec = pl.BlockSpec((1, DIFF_HEAD_DIM), lambda h, i: (0, 0))
    cast_specs = []
    for w in f32_weights:
        block = _cast_block_rows(w.shape[0], n_steps)
        n_blocks = w.shape[0] // block
        cast_specs.append(pl.BlockSpec(
            (block, w.shape[1]), lambda h, i, last=n_blocks - 1: (jnp.minimum(h * n_q + i, last), 0)))
    outs = pl.pallas_call(
        functools.partial(_attn_kernel, tkc=tkc, n_cast=len(f32_weights)),
        grid=(N_DIFF_HEADS, n_q),
        in_specs=[
            lam_spec, lam_spec, lam_spec, lam_spec,
            pl.BlockSpec((tq, DIFF_V_DIM), lambda h, i: (i, h)),
            pl.BlockSpec((seq, DIFF_V_DIM), lambda h, i: (0, h)),
            pl.BlockSpec((DIFF_V_DIM, seq), lambda h, i: (h, 0)),
            pl.BlockSpec((1, DIFF_V_DIM), lambda h, i: (0, 0)),
            pl.BlockSpec(norm_sq_max.shape, lambda h, i: (0, 0, 0)),
        ] + cast_specs,
        out_specs=[pl.BlockSpec((tq, DIFF_V_DIM), lambda h, i: (i, h))] + cast_specs,
        out_shape=[jax.ShapeDtypeStruct((seq, N_DIFF_HEADS * DIFF_V_DIM), BF16)]
        + [jax.ShapeDtypeStruct(w.shape, BF16) for w in f32_weights],
        scratch_shapes=[pltpu.VMEM((seq // tkc, tkc, 2 * tq), BF16),
                        pltpu.VMEM((F32_SUBLANES, 2 * tq), F32),
                        pltpu.VMEM((F32_SUBLANES, 2 * tq), F32),
                        pltpu.VMEM((DIFF_V_DIM, tq), F32)],
        compiler_params=_params(("arbitrary", "arbitrary")),
        name="diff_attention",
    )(lq1, lk1, lq2, lk2, q, k, v_t, subln_g, norm_sq_max, *f32_weights)
    return outs[0], outs[1:]


def _outproj_kernel(a_ref, gm_ref, w_ref, x_ref, gpost_ref, gffn_ref, x1_ref, h2_ref, *, row_block):
    tm, half = a_ref.shape
    for r in range(0, tm, row_block):
        rows = slice(r, r + row_block)
        mix = (jnp.dot(a_ref[rows, :], w_ref[0:half, :], preferred_element_type=F32)
               + jnp.dot(gm_ref[rows, :], w_ref[half:2 * half, :], preferred_element_type=F32))
        x1 = x_ref[rows, :] + _rms(mix, RMS_EPS) * gpost_ref[...]
        x1_ref[rows, :] = x1
        h2_ref[rows, :] = (_rms(x1, RMS_EPS) * gffn_ref[...]).astype(BF16)


def _outproj(attn, gm, w_out, x, post_mix_g, pre_ffn_g):
    seq, d_model = x.shape
    half = attn.shape[1]
    tm = 512
    row_spec = lambda cols: pl.BlockSpec((tm, cols), lambda i: (i, 0))
    const_spec = lambda shape: pl.BlockSpec(shape, lambda i: (0, 0))
    return pl.pallas_call(
        functools.partial(_outproj_kernel, row_block=128),
        grid=(seq // tm,),
        in_specs=[row_spec(half), row_spec(half), const_spec(w_out.shape), row_spec(d_model),
                  const_spec((1, d_model)), const_spec((1, d_model))],
        out_specs=[row_spec(d_model), row_spec(d_model)],
        out_shape=[jax.ShapeDtypeStruct((seq, d_model), F32), jax.ShapeDtypeStruct((seq, d_model), BF16)],
        compiler_params=_params(("arbitrary",)),
        name="outproj",
    )(attn, gm, w_out, x, post_mix_g, pre_ffn_g)


def _ffn_kernel(h_ref, wg_ref, wu_ref, wd_ref, x1_hbm, g_ref, o_ref, x1_buf, x1_sem):
    i = pl.program_id(0)
    j = pl.program_id(1)
    tm = o_ref.shape[0]
    x1_copy = pltpu.make_async_copy(x1_hbm.at[pl.ds(pl.multiple_of(i * tm, tm), tm), :], x1_buf, x1_sem)

    def chunk_output(rows):
        h = h_ref[rows, :]
        gate = jnp.dot(h, wg_ref[...], preferred_element_type=F32)
        up = jnp.dot(h, wu_ref[...], preferred_element_type=F32)
        act = (gate * jax.nn.sigmoid(gate) * up).astype(BF16)
        return jnp.dot(act, wd_ref[...], preferred_element_type=F32)

    def accumulate(rows):
        o_ref[rows, :] += chunk_output(rows)

    last = pl.num_programs(1) - 1

    @pl.when(j == 0)
    def _():
        x1_copy.start()
        o_ref[...] = chunk_output(slice(0, tm))

    @pl.when(jnp.logical_and(j > 0, j < last))
    def _():
        accumulate(slice(0, tm))

    @pl.when(j == last)
    def _():
        x1_copy.wait()
        for r in range(0, tm, tm // 2):
            rows = slice(r, r + tm // 2)
            accumulate(rows)
            o_ref[rows, :] = x1_buf[rows, :] + _rms(o_ref[rows, :], RMS_EPS) * g_ref[...]


def _ffn(h2, w_gate, w_up, w_down, x1, post_ffn_g):
    seq, d_model = x1.shape
    d_ff = w_gate.shape[1]
    tm = 1024
    tf = 512
    return pl.pallas_call(
        _ffn_kernel,
        grid=(seq // tm, d_ff // tf),
        in_specs=[
            pl.BlockSpec((tm, d_model), lambda i, j: (i, 0)),
            pl.BlockSpec((d_model, tf), lambda i, j: (0, j)),
            pl.BlockSpec((d_model, tf), lambda i, j: (0, j)),
            pl.BlockSpec((tf, d_model), lambda i, j: (j, 0)),
            pl.BlockSpec(memory_space=pl.ANY),
            pl.BlockSpec((1, d_model), lambda i, j: (0, 0)),
        ],
        out_specs=pl.BlockSpec((tm, d_model), lambda i, j: (i, 0)),
        out_shape=jax.ShapeDtypeStruct((seq, d_model), F32),
        scratch_shapes=[pltpu.VMEM((tm, d_model), F32), pltpu.SemaphoreType.DMA(())],
        compiler_params=_params(("arbitrary", "arbitrary")),
        name="ffn",
    )(h2, w_gate, w_up, w_down, x1, post_ffn_g)


def kernel(x, positions, pre_mix_g, w_in, lambda_q1, lambda_k1, lambda_q2, lambda_k2, subln_g,
           gmlp_ln_g, gmlp_ln_b, w_s, b_s, w_out, post_mix_g, pre_ffn_g, w_gate, w_up, w_down,
           post_ffn_g):
    batch, seq, d_model = x.shape
    assert batch == 1 and pre_mix_g.shape[0] == 1, "single sequence, single layer"
    assert seq % 1024 == 0 and d_model == 2048
    x2 = x[0]
    inv_freq = ROPE_THETA ** (-jnp.arange(0, DIFF_HEAD_DIM, 2, dtype=F32) / DIFF_HEAD_DIM)
    inv_lanes = jnp.tile(inv_freq, LANES // (DIFF_HEAD_DIM // 2))[None, :]
    bs_full = jnp.repeat(b_s[0].T, LANES, axis=1)
    q, k, v_t, gm, norm_sq_max = _inproj(x2, pre_mix_g, w_in[0].astype(BF16), positions[0][:, None], inv_lanes,
                          gmlp_ln_g, gmlp_ln_b, w_s[0].astype(BF16), bs_full)
    attn, (w_out16, w_gate16, w_up16, w_down16) = _attention(
        lambda_q1, lambda_k1, lambda_q2, lambda_k2, q, k, v_t, subln_g, norm_sq_max,
        (w_out[0], w_gate[0], w_up[0], w_down[0]))
    x1, h2 = _outproj(attn, gm, w_out16, x2, post_mix_g, pre_ffn_g)
    out = _ffn(h2, w_gate16, w_up16, w_down16, x1, post_ffn_g)
    return out[None]
```

```python
import functools
import math

import jax
import jax.numpy as jnp
from jax import lax
from jax.experimental import pallas as pl
from jax.experimental.pallas import tpu as pltpu

N_DIFF_HEADS = 8
DIFF_HEAD_DIM = 64
DIFF_V_DIM = 2 * DIFF_HEAD_DIM
N_GMLP_GROUPS = 8
CHUNK = 128
ROPE_THETA = 10000.0
RMS_EPS = 1e-6
LN_EPS = 1e-5
SUBLN_EPS = 1e-5
LAMBDA_INIT = 0.8 - 0.6 * math.exp(-0.3 * 0)

LANES = 128
VMEM_LIMIT_BYTES = 60 * 1024 * 1024

F32 = jnp.float32
BF16 = jnp.bfloat16


def _rms(x, eps):
    return x * lax.rsqrt(jnp.mean(x * x, axis=-1, keepdims=True) + eps)


def _gelu(x):
    return 0.5 * x * (1.0 + lax.erf(x * (1.0 / math.sqrt(2.0))))


def _params(semantics):
    return pltpu.CompilerParams(dimension_semantics=semantics, vmem_limit_bytes=VMEM_LIMIT_BYTES)


def _first_half_lanes(shape):
    lane = lax.broadcasted_iota(jnp.int32, shape, 1)
    return (lane % DIFF_HEAD_DIM) < (DIFF_HEAD_DIM // 2)


def _rope(x, cos, sin_signed):
    half = DIFF_HEAD_DIM // 2
    partner = jnp.where(_first_half_lanes(x.shape), pltpu.roll(x, LANES - half, 1), pltpu.roll(x, half, 1))
    return x * cos + partner * sin_signed


def _inproj_kernel(x_ref, g_ref, w32_ref, pos_ref, inv_ref, lng_ref, lnb_ref, ws_ref, bs_ref,
                   q_ref, k_ref, vt_ref, gm_ref, nmax_ref, h_ref, gu_ref, w_ref, *, q_scale, n_warm):
    step = pl.program_id(0)
    section = w32_ref.shape[1]
    for s in range(n_warm):
        @pl.when(step == s)
        def _(s=s):
            w_ref[:, s * section:(s + 1) * section] = w32_ref[...].astype(BF16)

    @pl.when(step >= n_warm)
    def _():
        _inproj_block(x_ref, g_ref, w_ref, pos_ref, inv_ref, lng_ref, lnb_ref, ws_ref, bs_ref,
                      q_ref, k_ref, vt_ref, gm_ref, nmax_ref, h_ref, gu_ref, q_scale=q_scale)


def _inproj_block(x_ref, g_ref, w_ref, pos_ref, inv_ref, lng_ref, lnb_ref, ws_ref, bs_ref,
                  q_ref, k_ref, vt_ref, gm_ref, nmax_ref, h_ref, gu_ref, *, q_scale):
    tm, width = q_ref.shape
    q_cols, k_cols, v_cols, gu_cols, gv_cols = (slice(n * width, (n + 1) * width) for n in range(5))
    def proj(cols):
        return jnp.dot(h_ref[...], w_ref[:, cols], preferred_element_type=F32)

    h_ref[...] = (_rms(x_ref[...], RMS_EPS) * g_ref[...]).astype(BF16)
    vt_ref[...] = proj(v_cols).T.astype(BF16)

    ang = pos_ref[...].astype(F32) * inv_ref[...]
    sin = jnp.sin(ang)
    sin = jnp.where(_first_half_lanes(ang.shape), -sin, sin)
    cos = jnp.cos(ang)

    map1_lanes = lax.broadcasted_iota(jnp.int32, (tm, LANES), 1) < DIFF_HEAD_DIM
    nmax_lane = lax.broadcasted_iota(jnp.int32, nmax_ref.shape[1:], 1)

    def rope_store(out_ref, cols, scale, nmax, first_lane):
        acc = proj(cols)
        for c in range(width // LANES):
            sl = slice(c * LANES, (c + 1) * LANES)
            rotated = (_rope(acc[:, sl], cos, sin) * scale).astype(BF16)
            out_ref[:, sl] = rotated
            sq = rotated.astype(F32)
            sq = sq * sq
            for m, in_map in enumerate((map1_lanes, jnp.logical_not(map1_lanes))):
                norm_sq = jnp.sum(jnp.where(in_map, sq, 0.0), axis=1, keepdims=True)
                nmax = jnp.where(nmax_lane == first_lane + 2 * c + m,
                                 jnp.max(norm_sq, axis=0, keepdims=True), nmax)
        return nmax

    nmax = rope_store(q_ref, q_cols, q_scale, jnp.zeros(nmax_ref.shape[1:], F32), 0)
    nmax_ref[0] = rope_store(k_ref, k_cols, 1.0, nmax, 2 * N_DIFF_HEADS)

    gu_ref[...] = _gelu(proj(gu_cols))
    gv = _gelu(proj(gv_cols))
    mu = jnp.mean(gv, axis=-1, keepdims=True)
    d = gv - mu
    var = jnp.mean(d * d, axis=-1, keepdims=True)
    vln = (d * lax.rsqrt(var + LN_EPS) * lng_ref[...] + lnb_ref[...]).astype(BF16)
    for c in range(tm // CHUNK):
        rows = slice(c * CHUNK, (c + 1) * CHUNK)
        for g in range(N_GMLP_GROUPS):
            cols = slice(g * LANES, (g + 1) * LANES)
            y = jnp.dot(ws_ref[g], vln[rows, cols], preferred_element_type=F32) + bs_ref[:, cols]
            gm_ref[rows, cols] = (gu_ref[rows, cols] * y).astype(BF16)


def _inproj(x, pre_g, w_in, positions, inv_lanes, ln_g, ln_b, w_s, bs_full):
    seq, d_model = x.shape
    width = w_in.shape[1] // 5
    tm = 512
    section = 512
    n_warm = w_in.shape[1] // section
    block = lambda i: jnp.maximum(i - n_warm, 0)
    row_spec = lambda cols: pl.BlockSpec((tm, cols), lambda i: (block(i), 0))
    const_spec = lambda shape, **kw: pl.BlockSpec(shape, lambda i: (0,) * len(shape), **kw)
    return pl.pallas_call(
        functools.partial(_inproj_kernel, q_scale=DIFF_HEAD_DIM ** -0.5 * math.log2(math.e), n_warm=n_warm),
        grid=(n_warm + seq // tm,),
        in_specs=[
            row_spec(d_model),
            const_spec((1, d_model)),
            pl.BlockSpec((d_model, section), lambda i: (0, jnp.minimum(i, n_warm - 1))),
            row_spec(1),
            const_spec((1, LANES)),
            const_spec((1, width)),
            const_spec((1, width)),
            const_spec(w_s.shape),
            const_spec(bs_full.shape),
        ],
        out_specs=[row_spec(width), row_spec(width), pl.BlockSpec((width, tm), lambda i: (0, block(i))),
                   row_spec(width), pl.BlockSpec((1, 8, LANES), lambda i: (block(i), 0, 0))],
        out_shape=[jax.ShapeDtypeStruct((seq, width), BF16), jax.ShapeDtypeStruct((seq, width), BF16),
                   jax.ShapeDtypeStruct((width, seq), BF16), jax.ShapeDtypeStruct((seq, width), BF16),
                   jax.ShapeDtypeStruct((seq // tm, 8, LANES), F32)],
        scratch_shapes=[pltpu.VMEM((tm, d_model), BF16), pltpu.VMEM((tm, width), F32),
                        pltpu.VMEM(w_in.shape, BF16)],
        compiler_params=_params(("arbitrary",)),
        name="inproj",
    )(x, pre_g, w_in, positions, inv_lanes, ln_g, ln_b, w_s, bs_full)


def _attn_kernel(*refs, tkc, n_cast):
    lq1_ref, lk1_ref, lq2_ref, lk2_ref, q_ref, k_ref, vt_ref, g_ref, nmax_ref = refs[:9]
    cast_src = refs[9:9 + n_cast]
    o_ref = refs[9 + n_cast]
    cast_dst = refs[10 + n_cast:10 + 2 * n_cast]
    p_ref, shift_ref, lpart_ref, acc_ref = refs[10 + 2 * n_cast:]

    for src, dst in zip(cast_src, cast_dst):
        dst[...] = src[...].astype(BF16)

    tq = q_ref.shape[0]
    two_tq = 2 * tq
    nkc = k_ref.shape[0] // tkc
    lane = lax.broadcasted_iota(jnp.int32, (tq, LANES), 1)
    in_map1 = lane < DIFF_HEAD_DIM
    q = q_ref[...]
    zero = jnp.zeros_like(q)
    q_maps = jnp.concatenate([jnp.where(in_map1, q, zero), jnp.where(in_map1, zero, q)], axis=0)
    nt_dims = (((1,), (1,)), ((), ()))
    col_tiles = [slice(c, c + LANES) for c in range(0, two_tq, LANES)]

    def key_rows(kc):
        if isinstance(kc, int):
            return slice(kc * tkc, (kc + 1) * tkc)
        return pl.ds(pl.multiple_of(kc * tkc, tkc), tkc)

    def score_tiles(kc):
        s = lax.dot_general(k_ref[key_rows(kc), :], q_maps, nt_dims, preferred_element_type=F32)
        return [s[:, cols].reshape(tkc // 8, 8, LANES) for cols in col_tiles]

    def shifted_probs(kc):
        for cols, tile in zip(col_tiles, score_tiles(kc)):
            p = jnp.exp2(tile - shift_ref[:, cols])
            p_ref[kc, :, cols] = p.reshape(tkc, LANES).astype(BF16)
            lpart_ref[:, cols] += jnp.sum(p, axis=0)

    def column_max(kc):
        for cols, tile in zip(col_tiles, score_tiles(kc)):
            lpart_ref[:, cols] = jnp.maximum(lpart_ref[:, cols], jnp.max(tile, axis=0))

    def sublane_reduce(op):
        return op(lpart_ref[...], axis=0, keepdims=True)

    norm_sq_max = jnp.max(nmax_ref[...], axis=0)
    nmax_lane = lax.broadcasted_iota(jnp.int32, norm_sq_max.shape, 1)
    head_lane = 2 * pl.program_id(0)

    def norm_sq(lane_index):
        picked = jnp.where(nmax_lane == lane_index, norm_sq_max, 0.0)
        return jnp.max(jnp.max(picked, axis=1, keepdims=True), axis=0, keepdims=True)

    for mp in range(2):
        bound = jnp.sqrt(norm_sq(head_lane + mp) * norm_sq(2 * N_DIFF_HEADS + head_lane + mp))
        shift_ref[:, mp * tq:(mp + 1) * tq] = jnp.broadcast_to(bound, (8, tq))
    lpart_ref[...] = jnp.zeros_like(lpart_ref)
    for kc in range(nkc):
        shifted_probs(kc)
    bounded_sums = sublane_reduce(jnp.sum)

    def redo_with_column_max():
        lpart_ref[...] = jnp.full(lpart_ref.shape, -jnp.inf, F32)
        lax.fori_loop(0, nkc, lambda kc, c: (column_max(kc), c)[1], 0)
        shift_ref[...] = jnp.broadcast_to(sublane_reduce(jnp.max), shift_ref.shape)
        lpart_ref[...] = jnp.zeros_like(lpart_ref)
        lax.fori_loop(0, nkc, lambda kc, c: (shifted_probs(kc), c)[1], 0)
        return sublane_reduce(jnp.sum)

    sums = lax.cond(jnp.min(bounded_sums) >= 2.0 ** -60, lambda: bounded_sums, redo_with_column_max)
    l1 = sums[:, :tq]
    l2 = sums[:, tq:]

    lam = (jnp.exp(jnp.sum(lq1_ref[...] * lk1_ref[...], axis=1, keepdims=True))
           - jnp.exp(jnp.sum(lq2_ref[...] * lk2_ref[...], axis=1, keepdims=True)) + LAMBDA_INIT)
    ratio = jnp.broadcast_to(lam * l1 / l2, (BF16_SUBLANES, tq)).astype(BF16)
    acc_ref[...] = jnp.zeros_like(acc_ref)
    for kc in range(nkc):
        p1 = p_ref[kc, :, 0:tq].reshape(tkc // BF16_SUBLANES, BF16_SUBLANES, tq)
        p2 = p_ref[kc, :, tq:two_tq].reshape(tkc // BF16_SUBLANES, BF16_SUBLANES, tq)
        w = (p1 - p2 * ratio).reshape(tkc, tq)
        acc_ref[...] += jnp.dot(vt_ref[:, kc * tkc:(kc + 1) * tkc], w, preferred_element_type=F32)

    out_t = acc_ref[...] * (1.0 / l1)
    out_t = out_t * lax.rsqrt(jnp.mean(out_t * out_t, axis=0, keepdims=True) + SUBLN_EPS)
    o_ref[...] = (out_t.T * g_ref[...] * (1.0 - LAMBDA_INIT)).astype(BF16)


BF16_SUBLANES = 16


def _cast_block_rows(rows, n_steps):
    for block in range(BF16_SUBLANES, rows + 1, BF16_SUBLANES):
        if rows % block == 0 and rows // block <= n_steps:
            return block
    raise ValueError(f"no row block for {rows} rows in {n_steps} steps")


def _attention(lq1, lk1, lq2, lk2, q, k, v_t, subln_g, norm_sq_max, f32_weights):
    seq = q.shape[0]
    tq = 1024
    tkc = 512
    n_q = seq // tq
    n_steps = N_DIFF_HEADS * n_q
    lam_spec = pl.BlockSpec((1, DIFF_HEAD_DIM), lambda h, i: (0, 0))
    cast_specs = []
    for w in f32_weights:
        block = _cast_block_rows(w.shape[0], n_steps)
        n_blocks = w.shape[0] // block
        cast_specs.append(pl.BlockSpec(
            (block, w.shape[1]), lambda h, i, last=n_blocks - 1: (jnp.minimum(h * n_q + i, last), 0)))
    outs = pl.pallas_call(
        functools.partial(_attn_kernel, tkc=tkc, n_cast=len(f32_weights)),
        grid=(N_DIFF_HEADS, n_q),
        in_specs=[
            lam_spec, lam_spec, lam_spec, lam_spec,
            pl.BlockSpec((tq, DIFF_V_DIM), lambda h, i: (i, h)),
            pl.BlockSpec((seq, DIFF_V_DIM), lambda h, i: (0, h)),
            pl.BlockSpec((DIFF_V_DIM, seq), lambda h, i: (h, 0)),
            pl.BlockSpec((1, DIFF_V_DIM), lambda h, i: (0, 0)),
            pl.BlockSpec(norm_sq_max.shape, lambda h, i: (0, 0, 0)),
        ] + cast_specs,
        out_specs=[pl.BlockSpec((tq, DIFF_V_DIM), lambda h, i: (i, h))] + cast_specs,
        out_shape=[jax.ShapeDtypeStruct((seq, N_DIFF_HEADS * DIFF_V_DIM), BF16)]
        + [jax.ShapeDtypeStruct(w.shape, BF16) for w in f32_weights],
        scratch_shapes=[pltpu.VMEM((seq // tkc, tkc, 2 * tq), BF16),
                        pltpu.VMEM((8, 2 * tq), F32),
                        pltpu.VMEM((8, 2 * tq), F32),
                        pltpu.VMEM((DIFF_V_DIM, tq), F32)],
        compiler_params=_params(("arbitrary", "arbitrary")),
        name="diff_attention",
    )(lq1, lk1, lq2, lk2, q, k, v_t, subln_g, norm_sq_max, *f32_weights)
    return outs[0], outs[1:]


def _outproj_kernel(a_ref, gm_ref, w_ref, x_ref, gpost_ref, gffn_ref, x1_ref, h2_ref, *, row_block):
    tm, half = a_ref.shape
    for r in range(0, tm, row_block):
        rows = slice(r, r + row_block)
        mix = (jnp.dot(a_ref[rows, :], w_ref[0:half, :], preferred_element_type=F32)
               + jnp.dot(gm_ref[rows, :], w_ref[half:2 * half, :], preferred_element_type=F32))
        x1 = x_ref[rows, :] + _rms(mix, RMS_EPS) * gpost_ref[...]
        x1_ref[rows, :] = x1
        h2_ref[rows, :] = (_rms(x1, RMS_EPS) * gffn_ref[...]).astype(BF16)


def _outproj(attn, gm, w_out, x, post_mix_g, pre_ffn_g):
    seq, d_model = x.shape
    half = attn.shape[1]
    tm = 512
    row_spec = lambda cols: pl.BlockSpec((tm, cols), lambda i: (i, 0))
    const_spec = lambda shape: pl.BlockSpec(shape, lambda i: (0, 0))
    return pl.pallas_call(
        functools.partial(_outproj_kernel, row_block=128),
        grid=(seq // tm,),
        in_specs=[row_spec(half), row_spec(half), const_spec(w_out.shape), row_spec(d_model),
                  const_spec((1, d_model)), const_spec((1, d_model))],
        out_specs=[row_spec(d_model), row_spec(d_model)],
        out_shape=[jax.ShapeDtypeStruct((seq, d_model), F32), jax.ShapeDtypeStruct((seq, d_model), BF16)],
        compiler_params=_params(("arbitrary",)),
        name="outproj",
    )(attn, gm, w_out, x, post_mix_g, pre_ffn_g)


def _ffn_kernel(h_ref, wg_ref, wu_ref, wd_ref, x1_hbm, g_ref, o_ref, x1_buf, x1_sem):
    i = pl.program_id(0)
    j = pl.program_id(1)
    tm = o_ref.shape[0]
    x1_copy = pltpu.make_async_copy(x1_hbm.at[pl.ds(pl.multiple_of(i * tm, tm), tm), :], x1_buf, x1_sem)

    @pl.when(j == 0)
    def _():
        x1_copy.start()
        o_ref[...] = jnp.zeros_like(o_ref)

    def accumulate(rows):
        h = h_ref[rows, :]
        gate = jnp.dot(h, wg_ref[...], preferred_element_type=F32)
        up = jnp.dot(h, wu_ref[...], preferred_element_type=F32)
        act = (gate * jax.nn.sigmoid(gate) * up).astype(BF16)
        o_ref[rows, :] += jnp.dot(act, wd_ref[...], preferred_element_type=F32)

    last = pl.num_programs(1) - 1

    @pl.when(j < last)
    def _():
        accumulate(slice(0, tm))

    @pl.when(j == last)
    def _():
        x1_copy.wait()
        for r in range(0, tm, tm // 2):
            rows = slice(r, r + tm // 2)
            accumulate(rows)
            o_ref[rows, :] = x1_buf[rows, :] + _rms(o_ref[rows, :], RMS_EPS) * g_ref[...]


def _ffn(h2, w_gate, w_up, w_down, x1, post_ffn_g):
    seq, d_model = x1.shape
    d_ff = w_gate.shape[1]
    tm = 1024
    tf = 512
    return pl.pallas_call(
        _ffn_kernel,
        grid=(seq // tm, d_ff // tf),
        in_specs=[
            pl.BlockSpec((tm, d_model), lambda i, j: (i, 0)),
            pl.BlockSpec((d_model, tf), lambda i, j: (0, j)),
            pl.BlockSpec((d_model, tf), lambda i, j: (0, j)),
            pl.BlockSpec((tf, d_model), lambda i, j: (j, 0)),
            pl.BlockSpec(memory_space=pl.ANY),
            pl.BlockSpec((1, d_model), lambda i, j: (0, 0)),
        ],
        out_specs=pl.BlockSpec((tm, d_model), lambda i, j: (i, 0)),
        out_shape=jax.ShapeDtypeStruct((seq, d_model), F32),
        scratch_shapes=[pltpu.VMEM((tm, d_model), F32), pltpu.SemaphoreType.DMA(())],
        compiler_params=_params(("arbitrary", "arbitrary")),
        name="ffn",
    )(h2, w_gate, w_up, w_down, x1, post_ffn_g)


def kernel(x, positions, pre_mix_g, w_in, lambda_q1, lambda_k1, lambda_q2, lambda_k2, subln_g,
           gmlp_ln_g, gmlp_ln_b, w_s, b_s, w_out, post_mix_g, pre_ffn_g, w_gate, w_up, w_down,
           post_ffn_g):
    batch, seq, d_model = x.shape
    assert batch == 1 and pre_mix_g.shape[0] == 1, "single sequence, single layer"
    assert seq % 1024 == 0 and d_model == 2048
    x2 = x[0]
    inv_freq = ROPE_THETA ** (-jnp.arange(0, DIFF_HEAD_DIM, 2, dtype=F32) / DIFF_HEAD_DIM)
    inv_lanes = jnp.tile(inv_freq, LANES // (DIFF_HEAD_DIM // 2))[None, :]
    bs_full = jnp.repeat(b_s[0].T, LANES, axis=1)
    q, k, v_t, gm, norm_sq_max = _inproj(x2, pre_mix_g, w_in[0], positions[0][:, None], inv_lanes,
                          gmlp_ln_g, gmlp_ln_b, w_s[0].astype(BF16), bs_full)
    attn, (w_out16, w_gate16, w_up16, w_down16) = _attention(
        lambda_q1, lambda_k1, lambda_q2, lambda_k2, q, k, v_t, subln_g, norm_sq_max,
        (w_out[0], w_gate[0], w_up[0], w_down[0]))
    x1, h2 = _outproj(attn, gm, w_out16, x2, post_mix_g, pre_ffn_g)
    out = _ffn(h2, w_gate16, w_up16, w_down16, x1, post_ffn_g)
    return out[None]
```

```python
import functools
import math

import jax
import jax.numpy as jnp
from jax import lax
from jax.experimental import pallas as pl
from jax.experimental.pallas import tpu as pltpu

N_DIFF_HEADS = 8
DIFF_HEAD_DIM = 64
DIFF_V_DIM = 2 * DIFF_HEAD_DIM
N_GMLP_GROUPS = 8
CHUNK = 128
ROPE_THETA = 10000.0
RMS_EPS = 1e-6
LN_EPS = 1e-5
SUBLN_EPS = 1e-5
LAMBDA_INIT = 0.8 - 0.6 * math.exp(-0.3 * 0)

LANES = 128
VMEM_LIMIT_BYTES = 60 * 1024 * 1024

F32 = jnp.float32
BF16 = jnp.bfloat16


def _rms(x, eps):
    return x * lax.rsqrt(jnp.mean(x * x, axis=-1, keepdims=True) + eps)


def _gelu(x):
    return 0.5 * x * (1.0 + lax.erf(x * (1.0 / math.sqrt(2.0))))


def _params(semantics):
    return pltpu.CompilerParams(dimension_semantics=semantics, vmem_limit_bytes=VMEM_LIMIT_BYTES)


def _first_half_lanes(shape):
    lane = lax.broadcasted_iota(jnp.int32, shape, 1)
    return (lane % DIFF_HEAD_DIM) < (DIFF_HEAD_DIM // 2)


def _rope(x, cos, sin_signed):
    half = DIFF_HEAD_DIM // 2
    partner = jnp.where(_first_half_lanes(x.shape), pltpu.roll(x, LANES - half, 1), pltpu.roll(x, half, 1))
    return x * cos + partner * sin_signed


def _inproj_kernel(x_ref, g_ref, w32_ref, pos_ref, inv_ref, lng_ref, lnb_ref, ws_ref, bs_ref,
                   q_ref, k_ref, vt_ref, gm_ref, nmax_ref, h_ref, gu_ref, w_ref, *, q_scale, n_warm):
    step = pl.program_id(0)
    section = w32_ref.shape[1]
    for s in range(n_warm):
        @pl.when(step == s)
        def _(s=s):
            w_ref[:, s * section:(s + 1) * section] = w32_ref[...].astype(BF16)

    @pl.when(step >= n_warm)
    def _():
        _inproj_block(x_ref, g_ref, w_ref, pos_ref, inv_ref, lng_ref, lnb_ref, ws_ref, bs_ref,
                      q_ref, k_ref, vt_ref, gm_ref, nmax_ref, h_ref, gu_ref, q_scale=q_scale)


def _inproj_block(x_ref, g_ref, w_ref, pos_ref, inv_ref, lng_ref, lnb_ref, ws_ref, bs_ref,
                  q_ref, k_ref, vt_ref, gm_ref, nmax_ref, h_ref, gu_ref, *, q_scale):
    tm, width = q_ref.shape
    q_cols, k_cols, v_cols, gu_cols, gv_cols = (slice(n * width, (n + 1) * width) for n in range(5))
    def proj(cols):
        return jnp.dot(h_ref[...], w_ref[:, cols], preferred_element_type=F32)

    h_ref[...] = (_rms(x_ref[...], RMS_EPS) * g_ref[...]).astype(BF16)
    vt_ref[...] = proj(v_cols).T.astype(BF16)

    ang = pos_ref[...].astype(F32) * inv_ref[...]
    sin = jnp.sin(ang)
    sin = jnp.where(_first_half_lanes(ang.shape), -sin, sin)
    cos = jnp.cos(ang)

    map1_lanes = lax.broadcasted_iota(jnp.int32, (tm, LANES), 1) < DIFF_HEAD_DIM
    nmax_lane = lax.broadcasted_iota(jnp.int32, nmax_ref.shape[1:], 1)

    def rope_store(out_ref, cols, scale, nmax, first_lane):
        acc = proj(cols)
        for c in range(width // LANES):
            sl = slice(c * LANES, (c + 1) * LANES)
            rotated = (_rope(acc[:, sl], cos, sin) * scale).astype(BF16)
            out_ref[:, sl] = rotated
            sq = rotated.astype(F32)
            sq = sq * sq
            for m, in_map in enumerate((map1_lanes, jnp.logical_not(map1_lanes))):
                norm_sq = jnp.sum(jnp.where(in_map, sq, 0.0), axis=1, keepdims=True)
                nmax = jnp.where(nmax_lane == first_lane + 2 * c + m,
                                 jnp.max(norm_sq, axis=0, keepdims=True), nmax)
        return nmax

    nmax = rope_store(q_ref, q_cols, q_scale, jnp.zeros(nmax_ref.shape[1:], F32), 0)
    nmax_ref[0] = rope_store(k_ref, k_cols, 1.0, nmax, 2 * N_DIFF_HEADS)

    gu_ref[...] = _gelu(proj(gu_cols))
    gv = _gelu(proj(gv_cols))
    mu = jnp.mean(gv, axis=-1, keepdims=True)
    d = gv - mu
    var = jnp.mean(d * d, axis=-1, keepdims=True)
    vln = (d * lax.rsqrt(var + LN_EPS) * lng_ref[...] + lnb_ref[...]).astype(BF16)
    for c in range(tm // CHUNK):
        rows = slice(c * CHUNK, (c + 1) * CHUNK)
        for g in range(N_GMLP_GROUPS):
            cols = slice(g * LANES, (g + 1) * LANES)
            y = jnp.dot(ws_ref[g], vln[rows, cols], preferred_element_type=F32) + bs_ref[:, cols]
            gm_ref[rows, cols] = (gu_ref[rows, cols] * y).astype(BF16)


def _inproj(x, pre_g, w_in, positions, inv_lanes, ln_g, ln_b, w_s, bs_full):
    seq, d_model = x.shape
    width = w_in.shape[1] // 5
    tm = 512
    section = 512
    n_warm = w_in.shape[1] // section
    block = lambda i: jnp.maximum(i - n_warm, 0)
    row_spec = lambda cols: pl.BlockSpec((tm, cols), lambda i: (block(i), 0))
    const_spec = lambda shape, **kw: pl.BlockSpec(shape, lambda i: (0,) * len(shape), **kw)
    return pl.pallas_call(
        functools.partial(_inproj_kernel, q_scale=DIFF_HEAD_DIM ** -0.5 * math.log2(math.e), n_warm=n_warm),
        grid=(n_warm + seq // tm,),
        in_specs=[
            row_spec(d_model),
            const_spec((1, d_model)),
            pl.BlockSpec((d_model, section), lambda i: (0, jnp.minimum(i, n_warm - 1))),
            row_spec(1),
            const_spec((1, LANES)),
            const_spec((1, width)),
            const_spec((1, width)),
            const_spec(w_s.shape),
            const_spec(bs_full.shape),
        ],
        out_specs=[row_spec(width), row_spec(width), pl.BlockSpec((width, tm), lambda i: (0, block(i))),
                   row_spec(width), pl.BlockSpec((1, 8, LANES), lambda i: (block(i), 0, 0))],
        out_shape=[jax.ShapeDtypeStruct((seq, width), BF16), jax.ShapeDtypeStruct((seq, width), BF16),
                   jax.ShapeDtypeStruct((width, seq), BF16), jax.ShapeDtypeStruct((seq, width), BF16),
                   jax.ShapeDtypeStruct((seq // tm, 8, LANES), F32)],
        scratch_shapes=[pltpu.VMEM((tm, d_model), BF16), pltpu.VMEM((tm, width), F32),
                        pltpu.VMEM(w_in.shape, BF16)],
        compiler_params=_params(("arbitrary",)),
        name="inproj",
    )(x, pre_g, w_in, positions, inv_lanes, ln_g, ln_b, w_s, bs_full)


def _attn_kernel(*refs, tkc, n_cast):
    lq1_ref, lk1_ref, lq2_ref, lk2_ref, q_ref, k_ref, vt_ref, g_ref, nmax_ref = refs[:9]
    cast_src = refs[9:9 + n_cast]
    o_ref = refs[9 + n_cast]
    cast_dst = refs[10 + n_cast:10 + 2 * n_cast]
    p_ref, shift_ref, lpart_ref, acc_ref = refs[10 + 2 * n_cast:]

    tq = q_ref.shape[0]
    two_tq = 2 * tq
    nkc = k_ref.shape[0] // tkc
    lane = lax.broadcasted_iota(jnp.int32, (tq, LANES), 1)
    in_map1 = lane < DIFF_HEAD_DIM
    q = q_ref[...]
    zero = jnp.zeros_like(q)
    q_maps = jnp.concatenate([jnp.where(in_map1, q, zero), jnp.where(in_map1, zero, q)], axis=0)
    nt_dims = (((1,), (1,)), ((), ()))
    col_tiles = [slice(c, c + LANES) for c in range(0, two_tq, LANES)]

    def key_rows(kc):
        if isinstance(kc, int):
            return slice(kc * tkc, (kc + 1) * tkc)
        return pl.ds(pl.multiple_of(kc * tkc, tkc), tkc)

    def score_tiles(kc):
        s = lax.dot_general(k_ref[key_rows(kc), :], q_maps, nt_dims, preferred_element_type=F32)
        return [s[:, cols].reshape(tkc // 8, 8, LANES) for cols in col_tiles]

    def shifted_probs(kc):
        for cols, tile in zip(col_tiles, score_tiles(kc)):
            p = jnp.exp2(tile - shift_ref[:, cols])
            p_ref[kc, :, cols] = p.reshape(tkc, LANES).astype(BF16)
            lpart_ref[:, cols] += jnp.sum(p, axis=0)

    def column_max(kc):
        for cols, tile in zip(col_tiles, score_tiles(kc)):
            lpart_ref[:, cols] = jnp.maximum(lpart_ref[:, cols], jnp.max(tile, axis=0))

    def sublane_reduce(op):
        return op(lpart_ref[...], axis=0, keepdims=True)

    norm_sq_max = jnp.max(nmax_ref[...], axis=0)
    nmax_lane = lax.broadcasted_iota(jnp.int32, norm_sq_max.shape, 1)
    head_lane = 2 * pl.program_id(0)

    def norm_sq(lane_index):
        picked = jnp.where(nmax_lane == lane_index, norm_sq_max, 0.0)
        return jnp.max(jnp.max(picked, axis=1, keepdims=True), axis=0, keepdims=True)

    for mp in range(2):
        bound = jnp.sqrt(norm_sq(head_lane + mp) * norm_sq(2 * N_DIFF_HEADS + head_lane + mp))
        shift_ref[:, mp * tq:(mp + 1) * tq] = jnp.broadcast_to(bound, (8, tq))
    lpart_ref[...] = jnp.zeros_like(lpart_ref)
    for kc in range(nkc):
        shifted_probs(kc)
    bounded_sums = sublane_reduce(jnp.sum)

    def redo_with_column_max():
        lpart_ref[...] = jnp.full(lpart_ref.shape, -jnp.inf, F32)
        lax.fori_loop(0, nkc, lambda kc, c: (column_max(kc), c)[1], 0)
        shift_ref[...] = jnp.broadcast_to(sublane_reduce(jnp.max), shift_ref.shape)
        lpart_ref[...] = jnp.zeros_like(lpart_ref)
        lax.fori_loop(0, nkc, lambda kc, c: (shifted_probs(kc), c)[1], 0)
        return sublane_reduce(jnp.sum)

    sums = lax.cond(jnp.min(bounded_sums) >= 2.0 ** -60, lambda: bounded_sums, redo_with_column_max)
    l1 = sums[:, :tq]
    l2 = sums[:, tq:]

    for src, dst in zip(cast_src, cast_dst):
        dst[...] = src[...].astype(BF16)

    lam = (jnp.exp(jnp.sum(lq1_ref[...] * lk1_ref[...], axis=1, keepdims=True))
           - jnp.exp(jnp.sum(lq2_ref[...] * lk2_ref[...], axis=1, keepdims=True)) + LAMBDA_INIT)
    ratio = jnp.broadcast_to(lam * l1 / l2, (BF16_SUBLANES, tq)).astype(BF16)
    acc_ref[...] = jnp.zeros_like(acc_ref)
    for kc in range(nkc):
        p1 = p_ref[kc, :, 0:tq].reshape(tkc // BF16_SUBLANES, BF16_SUBLANES, tq)
        p2 = p_ref[kc, :, tq:two_tq].reshape(tkc // BF16_SUBLANES, BF16_SUBLANES, tq)
        w = (p1 - p2 * ratio).reshape(tkc, tq)
        acc_ref[...] += jnp.dot(vt_ref[:, kc * tkc:(kc + 1) * tkc], w, preferred_element_type=F32)

    out_t = acc_ref[...] * (1.0 / l1)
    out_t = out_t * lax.rsqrt(jnp.mean(out_t * out_t, axis=0, keepdims=True) + SUBLN_EPS)
    o_ref[...] = (out_t.T * g_ref[...] * (1.0 - LAMBDA_INIT)).astype(BF16)


BF16_SUBLANES = 16


def _cast_block_rows(rows, n_steps):
    for block in range(BF16_SUBLANES, rows + 1, BF16_SUBLANES):
        if rows % block == 0 and rows // block <= n_steps:
            return block
    raise ValueError(f"no row block for {rows} rows in {n_steps} steps")


def _attention(lq1, lk1, lq2, lk2, q, k, v_t, subln_g, norm_sq_max, f32_weights):
    seq = q.shape[0]
    tq = 1024
    tkc = 512
    n_q = seq // tq
    n_steps = N_DIFF_HEADS * n_q
    lam_spec = pl.BlockSpec((1, DIFF_HEAD_DIM), lambda h, i: (0, 0))
    cast_specs = []
    for w in f32_weights:
        block = _cast_block_rows(w.shape[0], n_steps)
        n_blocks = w.shape[0] // block
        cast_specs.append(pl.BlockSpec(
            (block, w.shape[1]), lambda h, i, last=n_blocks - 1: (jnp.minimum(h * n_q + i, last), 0)))
    outs = pl.pallas_call(
        functools.partial(_attn_kernel, tkc=tkc, n_cast=len(f32_weights)),
        grid=(N_DIFF_HEADS, n_q),
        in_specs=[
            lam_spec, lam_spec, lam_spec, lam_spec,
            pl.BlockSpec((tq, DIFF_V_DIM), lambda h, i: (i, h)),
            pl.BlockSpec((seq, DIFF_V_DIM), lambda h, i: (0, h)),
            pl.BlockSpec((DIFF_V_DIM, seq), lambda h, i: (h, 0)),
            pl.BlockSpec((1, DIFF_V_DIM), lambda h, i: (0, 0)),
            pl.BlockSpec(norm_sq_max.shape, lambda h, i: (0, 0, 0)),
        ] + cast_specs,
        out_specs=[pl.BlockSpec((tq, DIFF_V_DIM), lambda h, i: (i, h))] + cast_specs,
        out_shape=[jax.ShapeDtypeStruct((seq, N_DIFF_HEADS * DIFF_V_DIM), BF16)]
        + [jax.ShapeDtypeStruct(w.shape, BF16) for w in f32_weights],
        scratch_shapes=[pltpu.VMEM((seq // tkc, tkc, 2 * tq), BF16),
                        pltpu.VMEM((8, 2 * tq), F32),
                        pltpu.VMEM((8, 2 * tq), F32),
                        pltpu.VMEM((DIFF_V_DIM, tq), F32)],
        compiler_params=_params(("arbitrary", "arbitrary")),
        name="diff_attention",
    )(lq1, lk1, lq2, lk2, q, k, v_t, subln_g, norm_sq_max, *f32_weights)
    return outs[0], outs[1:]


def _outproj_kernel(a_ref, gm_ref, w_ref, x_ref, gpost_ref, gffn_ref, x1_ref, h2_ref, *, row_block):
    tm, half = a_ref.shape
    for r in range(0, tm, row_block):
        rows = slice(r, r + row_block)
        mix = (jnp.dot(a_ref[rows, :], w_ref[0:half, :], preferred_element_type=F32)
               + jnp.dot(gm_ref[rows, :], w_ref[half:2 * half, :], preferred_element_type=F32))
        x1 = x_ref[rows, :] + _rms(mix, RMS_EPS) * gpost_ref[...]
        x1_ref[rows, :] = x1
        h2_ref[rows, :] = (_rms(x1, RMS_EPS) * gffn_ref[...]).astype(BF16)


def _outproj(attn, gm, w_out, x, post_mix_g, pre_ffn_g):
    seq, d_model = x.shape
    half = attn.shape[1]
    tm = 512
    row_spec = lambda cols: pl.BlockSpec((tm, cols), lambda i: (i, 0))
    const_spec = lambda shape: pl.BlockSpec(shape, lambda i: (0, 0))
    return pl.pallas_call(
        functools.partial(_outproj_kernel, row_block=128),
        grid=(seq // tm,),
        in_specs=[row_spec(half), row_spec(half), const_spec(w_out.shape), row_spec(d_model),
                  const_spec((1, d_model)), const_spec((1, d_model))],
        out_specs=[row_spec(d_model), row_spec(d_model)],
        out_shape=[jax.ShapeDtypeStruct((seq, d_model), F32), jax.ShapeDtypeStruct((seq, d_model), BF16)],
        compiler_params=_params(("arbitrary",)),
        name="outproj",
    )(attn, gm, w_out, x, post_mix_g, pre_ffn_g)


def _ffn_kernel(h_ref, wg_ref, wu_ref, wd_ref, x1_hbm, g_ref, o_ref, x1_buf, x1_sem):
    i = pl.program_id(0)
    j = pl.program_id(1)
    tm = o_ref.shape[0]
    x1_copy = pltpu.make_async_copy(x1_hbm.at[pl.ds(pl.multiple_of(i * tm, tm), tm), :], x1_buf, x1_sem)

    @pl.when(j == 0)
    def _():
        x1_copy.start()
        o_ref[...] = jnp.zeros_like(o_ref)

    def accumulate(rows):
        h = h_ref[rows, :]
        gate = jnp.dot(h, wg_ref[...], preferred_element_type=F32)
        up = jnp.dot(h, wu_ref[...], preferred_element_type=F32)
        act = (gate * jax.nn.sigmoid(gate) * up).astype(BF16)
        o_ref[rows, :] += jnp.dot(act, wd_ref[...], preferred_element_type=F32)

    last = pl.num_programs(1) - 1

    @pl.when(j < last)
    def _():
        accumulate(slice(0, tm))

    @pl.when(j == last)
    def _():
        x1_copy.wait()
        for r in range(0, tm, tm // 2):
            rows = slice(r, r + tm // 2)
            accumulate(rows)
            o_ref[rows, :] = x1_buf[rows, :] + _rms(o_ref[rows, :], RMS_EPS) * g_ref[...]


def _ffn(h2, w_gate, w_up, w_down, x1, post_ffn_g):
    seq, d_model = x1.shape
    d_ff = w_gate.shape[1]
    tm = 1024
    tf = 512
    return pl.pallas_call(
        _ffn_kernel,
        grid=(seq // tm, d_ff // tf),
        in_specs=[
            pl.BlockSpec((tm, d_model), lambda i, j: (i, 0)),
            pl.BlockSpec((d_model, tf), lambda i, j: (0, j)),
            pl.BlockSpec((d_model, tf), lambda i, j: (0, j)),
            pl.BlockSpec((tf, d_model), lambda i, j: (j, 0)),
            pl.BlockSpec(memory_space=pl.ANY),
            pl.BlockSpec((1, d_model), lambda i, j: (0, 0)),
        ],
        out_specs=pl.BlockSpec((tm, d_model), lambda i, j: (i, 0)),
        out_shape=jax.ShapeDtypeStruct((seq, d_model), F32),
        scratch_shapes=[pltpu.VMEM((tm, d_model), F32), pltpu.SemaphoreType.DMA(())],
        compiler_params=_params(("arbitrary", "arbitrary")),
        name="ffn",
    )(h2, w_gate, w_up, w_down, x1, post_ffn_g)


def kernel(x, positions, pre_mix_g, w_in, lambda_q1, lambda_k1, lambda_q2, lambda_k2, subln_g,
           gmlp_ln_g, gmlp_ln_b, w_s, b_s, w_out, post_mix_g, pre_ffn_g, w_gate, w_up, w_down,
           post_ffn_g):
    batch, seq, d_model = x.shape
    assert batch == 1 and pre_mix_g.shape[0] == 1, "single sequence, single layer"
    assert seq % 1024 == 0 and d_model == 2048
    x2 = x[0]
    inv_freq = ROPE_THETA ** (-jnp.arange(0, DIFF_HEAD_DIM, 2, dtype=F32) / DIFF_HEAD_DIM)
    inv_lanes = jnp.tile(inv_freq, LANES // (DIFF_HEAD_DIM // 2))[None, :]
    bs_full = jnp.repeat(b_s[0].T, LANES, axis=1)
    q, k, v_t, gm, norm_sq_max = _inproj(x2, pre_mix_g, w_in[0], positions[0][:, None], inv_lanes,
                          gmlp_ln_g, gmlp_ln_b, w_s[0].astype(BF16), bs_full)
    attn, (w_out16, w_gate16, w_up16, w_down16) = _attention(
        lambda_q1, lambda_k1, lambda_q2, lambda_k2, q, k, v_t, subln_g, norm_sq_max,
        (w_out[0], w_gate[0], w_up[0], w_down[0]))
    x1, h2 = _outproj(attn, gm, w_out16, x2, post_mix_g, pre_ffn_g)
    out = _ffn(h2, w_gate16, w_up16, w_down16, x1, post_ffn_g)
    return out[None]
```
